```python
import jax, jax.numpy as jnp
from jax import lax
import numpy as np

D_MODEL = 1024
BATCH = 2
SEQ = 8192
DEPTH = 2
DEC_BATCH = 32
DEC_SEQ = 4
PAST_LEN = 16384
PAGE_SIZE = 128

CHUNK = 128
D_A = D_MODEL // 2
G_A = 8
GD_A = D_A // G_A
N_HEADS = 8
HEAD_DIM = 64
N_KV_HEADS = 4
N_IDX_HEADS = 8
IDX_DIM = 64
TOPK_MAX = 256
QBLK = 128
ROPE_THETA = 500000.0
D_FF = 2816
N_SUB = 3
EPS = 1e-6

SPLIT_SIZES = (D_A, D_A, N_HEADS * HEAD_DIM, N_KV_HEADS * HEAD_DIM, N_KV_HEADS * HEAD_DIM,
               N_IDX_HEADS * IDX_DIM, IDX_DIM, N_IDX_HEADS, D_MODEL, D_MODEL)
D_IN = int(sum(SPLIT_SIZES))
SPLIT_AT = tuple(int(s) for s in np.cumsum(SPLIT_SIZES)[:-1])

kernel_name = 'hybrid_chunkmlp_dsa_macaron_step'


def rmsnorm(x, g):
    x32 = x.astype(jnp.float32)
    y = x32 * lax.rsqrt(jnp.mean(x32 * x32, axis=-1, keepdims=True) + EPS)
    return (y * g.astype(jnp.float32)).astype(x.dtype)


def rope(x, pos):
    rd = x.shape[-1] // 4
    half = rd // 2
    freqs = ROPE_THETA ** (-jnp.arange(half, dtype=jnp.float32) * 2.0 / rd)
    ang = pos.astype(jnp.float32)[:, None] * freqs[None, :]
    cos = jnp.cos(ang)[None, :, None, :]
    sin = jnp.sin(ang)[None, :, None, :]
    xr = x[..., :rd].astype(jnp.float32)
    x1, x2 = xr[..., :half], xr[..., half:]
    rot = jnp.concatenate([x1 * cos - x2 * sin, x2 * cos + x1 * sin], axis=-1).astype(x.dtype)
    return jnp.concatenate([rot, x[..., rd:]], axis=-1)


def swiglu(h, w_in, w_out):
    a, b = jnp.split(h @ w_in, 2, axis=-1)
    return (jax.nn.silu(a) * b) @ w_out


def gather_rows(x, idx):
    return jax.vmap(lambda xb, ib: xb[ib])(x, idx)


def mixer_project(h, pos, lp):
    B, T = h.shape[:2]
    z = h @ lp['w_in']
    zu, zv, zq, zk, zva, zqi, zki, zwi, zga, zgb = jnp.split(z, SPLIT_AT, axis=-1)
    u = jax.nn.gelu(zu)
    v = rmsnorm(jax.nn.gelu(zv), lp['sgu_norm_g'])
    q = rope(rmsnorm(zq.reshape(B, T, N_HEADS, HEAD_DIM), lp['q_norm_g']), pos)
    k = rope(rmsnorm(zk.reshape(B, T, N_KV_HEADS, HEAD_DIM), lp['k_norm_g']), pos)
    va = zva.reshape(B, T, N_KV_HEADS, HEAD_DIM)
    qi = rope(zqi.reshape(B, T, N_IDX_HEADS, IDX_DIM), pos)
    ki = rope(zki[:, :, None, :], pos)[:, :, 0, :]
    wi = zwi * (N_IDX_HEADS ** -0.5)
    return u, v, q, k, va, qi, ki, wi, jax.nn.sigmoid(zga), jax.nn.sigmoid(zgb)


def chunk_spatial_gate(u, v, w_s, b_s):
    B, T, _ = v.shape
    r = min(T, CHUNK)
    nc = T // r
    wm = jnp.tril(w_s[:, :r, :r])
    vc = v.reshape(B, nc, r, G_A, GD_A)
    mixed = jnp.einsum('gts,bcsgd->bctgd', wm, vc) + b_s[:, :r].T[None, None, :, :, None]
    return u * mixed.reshape(B, T, D_A).astype(u.dtype)


def index_scores(qi, wi, ki):
    s = jax.nn.relu(jnp.einsum('bthd,bld->bthl', qi, ki).astype(jnp.float32))
    return jnp.einsum('bth,bthl->btl', wi.astype(jnp.float32), s)


def sparse_attend(q, ks, vs, valid):
    B, T = q.shape[:2]
    qg = q.reshape(B, T, N_KV_HEADS, N_HEADS // N_KV_HEADS, HEAD_DIM)
    s = jnp.einsum('btkgd,btskd->btkgs', qg, ks).astype(jnp.float32) * (HEAD_DIM ** -0.5)
    s = jnp.where(valid[:, :, None, None, :], s, -jnp.inf)
    p = jax.nn.softmax(s, axis=-1).astype(vs.dtype)
    o = jnp.einsum('btkgs,btskd->btkgd', p, vs)
    return o.reshape(B, T, N_HEADS * HEAD_DIM)


def prompt_attention(q, k, va, qi, ki, wi):
    B, S = q.shape[:2]
    nb = S // QBLK
    topk = min(TOPK_MAX, S // 4)
    key_pos = jnp.arange(S)

    def to_blocks(a):
        return a.reshape((B, nb, QBLK) + a.shape[2:]).swapaxes(0, 1)

    def block(args):
        i, qb, qib, wib = args
        t = i * QBLK + jnp.arange(QBLK)
        sc = index_scores(qib, wib, ki)
        sc = jnp.where((key_pos[None, :] <= t[:, None])[None], sc, -jnp.inf)
        _, idx = lax.top_k(sc, topk)
        valid = idx <= t[None, :, None]
        return sparse_attend(qb, gather_rows(k, idx), gather_rows(va, idx), valid)

    o = lax.map(block, (jnp.arange(nb), to_blocks(q), to_blocks(qi), to_blocks(wi)))
    return o.swapaxes(0, 1).reshape(B, S, N_HEADS * HEAD_DIM)


def sample_attention(q, k, va, qi, ki, wi, ck, cv, cik, page_table):
    DB, T = q.shape[:2]
    L = PAST_LEN + T
    topk = min(TOPK_MAX, L // 4)
    ki_past = cik[page_table].reshape(DB, PAST_LEN, IDX_DIM)
    ki_all = jnp.concatenate([ki_past, ki.astype(ki_past.dtype)], axis=1)
    t = PAST_LEN + jnp.arange(T)
    sc = index_scores(qi, wi, ki_all)
    sc = jnp.where((jnp.arange(L)[None, :] <= t[:, None])[None], sc, -jnp.inf)
    _, idx = lax.top_k(sc, topk)
    valid = idx <= t[None, :, None]
    s_past = jnp.minimum(idx, PAST_LEN - 1)
    phys = jax.vmap(lambda pt, i: pt[i])(page_table, s_past // PAGE_SIZE)
    off = s_past % PAGE_SIZE
    is_new = (idx >= PAST_LEN)[..., None, None]
    s_new = jnp.clip(idx - PAST_LEN, 0, T - 1)
    ks = jnp.where(is_new, gather_rows(k, s_new).astype(ck.dtype), ck[phys, off])
    vs = jnp.where(is_new, gather_rows(va, s_new).astype(cv.dtype), cv[phys, off])
    return sparse_attend(q, ks, vs, valid)


def merge_branches(o_a, o_b, ga, gb, lp):
    return (ga * (o_a @ lp['w_branch_a']) + gb * (o_b @ lp['w_branch_b'])) @ lp['w_out']


def mix_prompt(h, lp):
    pos = jnp.arange(h.shape[1])
    u, v, q, k, va, qi, ki, wi, ga, gb = mixer_project(h, pos, lp)
    o_a = chunk_spatial_gate(u, v, lp['sgu_w'], lp['sgu_b'])
    o_b = prompt_attention(q, k, va, qi, ki, wi)
    return merge_branches(o_a, o_b, ga, gb, lp), (k, va, ki)


def mix_sample(h, lp, ck, cv, cik, page_table):
    pos = PAST_LEN + jnp.arange(h.shape[1])
    u, v, q, k, va, qi, ki, wi, ga, gb = mixer_project(h, pos, lp)
    o_a = chunk_spatial_gate(u, v, lp['sgu_w'], lp['sgu_b'])
    o_b = sample_attention(q, k, va, qi, ki, wi, ck, cv, cik, page_table)
    return merge_branches(o_a, o_b, ga, gb, lp), (k, va, ki, v)


def layer(x, c, lp, mixer):
    mod = jax.nn.silu(c) @ lp['mod_w'] + lp['mod_b']
    mod = mod.reshape(c.shape[0], N_SUB, 3, D_MODEL)[:, :, :, None, :]

    def adaln(y, j):
        return rmsnorm(y, lp['norm_g'][j]) * (1.0 + mod[:, j, 1]) + mod[:, j, 0]

    x = x + 0.5 * mod[:, 0, 2] * swiglu(adaln(x, 0), lp['ffn_w_in'][0], lp['ffn_w_out'][0])
    y, state = mixer(adaln(x, 1))
    x = x + mod[:, 1, 2] * y
    x = x + 0.5 * mod[:, 2, 2] * swiglu(adaln(x, 2), lp['ffn_w_in'][1], lp['ffn_w_out'][1])
    return x, state


def setup_inputs(seed: int = 0) -> dict:
    key = jax.random.key(seed)
    ks = jax.random.split(key, 24)
    n_pages = PAST_LEN // PAGE_SIZE
    n_used = DEC_BATCH * n_pages
    n_phys = n_used + max(1, n_used // 4)

    def nrm(k, shape, scale):
        return jax.random.normal(k, shape, jnp.float32) * scale

    page_table = jax.random.permutation(ks[7], n_phys)[:n_used].reshape(DEC_BATCH, n_pages).astype(jnp.int32)
    return {
        'x_prompt': nrm(ks[0], (BATCH, SEQ, D_MODEL), 1.0),
        'x_sample': nrm(ks[1], (DEC_BATCH, DEC_SEQ, D_MODEL), 1.0),
        'c_prompt': nrm(ks[2], (BATCH, D_MODEL), 1.0),
        'c_sample': nrm(ks[3], (DEC_BATCH, D_MODEL), 1.0),
        'cache_k': nrm(ks[4], (DEPTH, n_phys, PAGE_SIZE, N_KV_HEADS, HEAD_DIM), 1.0),
        'cache_v': nrm(ks[5], (DEPTH, n_phys, PAGE_SIZE, N_KV_HEADS, HEAD_DIM), 1.0),
        'cache_idx_k': nrm(ks[6], (DEPTH, n_phys, PAGE_SIZE, IDX_DIM), 1.0),
        'page_table': page_table,
        'mod_w': nrm(ks[8], (DEPTH, D_MODEL, N_SUB * 3 * D_MODEL), 0.5 * D_MODEL ** -0.5),
        'mod_b': nrm(ks[9], (DEPTH, N_SUB * 3 * D_MODEL), 0.01),
        'norm_g': 1.0 + nrm(ks[10], (DEPTH, N_SUB, D_MODEL), 0.02),
        'ffn_w_in': nrm(ks[11], (DEPTH, 2, D_MODEL, 2 * D_FF), D_MODEL ** -0.5),
        'ffn_w_out': nrm(ks[12], (DEPTH, 2, D_FF, D_MODEL), D_FF ** -0.5),
        'w_in': nrm(ks[13], (DEPTH, D_MODEL, D_IN), D_MODEL ** -0.5),
        'sgu_norm_g': 1.0 + nrm(ks[14], (DEPTH, D_A), 0.02),
        'sgu_w': nrm(ks[15], (DEPTH, G_A, CHUNK, CHUNK), CHUNK ** -0.5),
        'sgu_b': 1.0 + nrm(ks[16], (DEPTH, G_A, CHUNK), 0.02),
        'q_norm_g': 1.0 + nrm(ks[17], (DEPTH, HEAD_DIM), 0.02),
        'k_norm_g': 1.0 + nrm(ks[18], (DEPTH, HEAD_DIM), 0.02),
        'w_branch_a': nrm(ks[19], (DEPTH, D_A, D_MODEL), D_A ** -0.5),
        'w_branch_b': nrm(ks[20], (DEPTH, N_HEADS * HEAD_DIM, D_MODEL), (N_HEADS * HEAD_DIM) ** -0.5),
        'w_out': nrm(ks[21], (DEPTH, D_MODEL, D_MODEL), D_MODEL ** -0.5),
    }


def reference(x_prompt, x_sample, c_prompt, c_sample, cache_k, cache_v, cache_idx_k, page_table,
              mod_w, mod_b, norm_g, ffn_w_in, ffn_w_out, w_in, sgu_norm_g, sgu_w, sgu_b,
              q_norm_g, k_norm_g, w_branch_a, w_branch_b, w_out):
    yp, ys = x_prompt, x_sample
    kp_l, vp_l, ip_l, ks_l, vs_l, is_l, cv_l = [], [], [], [], [], [], []
    for l in range(DEPTH):
        lp = {'mod_w': mod_w[l], 'mod_b': mod_b[l], 'norm_g': norm_g[l],
              'ffn_w_in': ffn_w_in[l], 'ffn_w_out': ffn_w_out[l], 'w_in': w_in[l],
              'sgu_norm_g': sgu_norm_g[l], 'sgu_w': sgu_w[l], 'sgu_b': sgu_b[l],
              'q_norm_g': q_norm_g[l], 'k_norm_g': k_norm_g[l],
              'w_branch_a': w_branch_a[l], 'w_branch_b': w_branch_b[l], 'w_out': w_out[l]}
        yp, (kp, vp, ip) = layer(yp, c_prompt, lp, lambda h: mix_prompt(h, lp))
        ys, (kn, vn, inew, cvn) = layer(
            ys, c_sample, lp,
            lambda h: mix_sample(h, lp, cache_k[l], cache_v[l], cache_idx_k[l], page_table))
        kp_l.append(kp); vp_l.append(vp); ip_l.append(ip)
        ks_l.append(kn); vs_l.append(vn); is_l.append(inew); cv_l.append(cvn)
    new_k_prompt = jnp.stack(kp_l)
    new_v_prompt = jnp.stack(vp_l)
    new_idx_k_prompt = jnp.stack(ip_l)
    new_k_sample = jnp.stack(ks_l)
    new_v_sample = jnp.stack(vs_l)
    new_idx_k_sample = jnp.stack(is_l)
    new_chunk_v_sample = jnp.stack(cv_l)
    return (yp, ys, new_k_prompt, new_v_prompt, new_idx_k_prompt,
            new_k_sample, new_v_sample, new_idx_k_sample, new_chunk_v_sample)
```

```python
import collections
import functools

import jax
import jax.numpy as jnp
import numpy as np
from jax import lax
from jax.experimental import pallas as pl
from jax.experimental.pallas import tpu as pltpu

F32 = jnp.float32
BF16 = jnp.bfloat16
I32 = jnp.int32

LANES = 128
INT_MIN = -(2 ** 31)
NEG_BIG = -1e30
VMEM_LIMIT = 56 * 1024 * 1024

Config = collections.namedtuple(
    "Config",
    "d_model seq depth dec_seq past_len page_size chunk d_a g_a n_heads head_dim n_kv "
    "n_idx_heads idx_dim topk_max rope_theta d_ff n_sub eps")

CFG = Config(d_model=1024, seq=8192, depth=2, dec_seq=4, past_len=16384, page_size=128, chunk=128,
             d_a=512, g_a=8, n_heads=8, head_dim=64, n_kv=4, n_idx_heads=8, idx_dim=64,
             topk_max=256, rope_theta=500000.0, d_ff=2816, n_sub=3, eps=1e-6)


def _dot(a, b):
    return lax.dot_general(a, b, (((1,), (0,)), ((), ())), preferred_element_type=F32)


def _dot_nt(a, b):
    return lax.dot_general(a, b, (((1,), (1,)), ((), ())), preferred_element_type=F32)


def _const_spec(shape):
    nd = len(shape)
    return pl.BlockSpec(shape, lambda *_: (0,) * nd, pipeline_mode=pl.Buffered(1))


def _params(n_grid):
    return pltpu.CompilerParams(dimension_semantics=("arbitrary",) * n_grid,
                                vmem_limit_bytes=VMEM_LIMIT)


def _mod_kernel(c_ref, w_ref, b_ref, o_ref):
    c = c_ref[...]
    s = (c * jax.nn.sigmoid(c)).astype(BF16)
    o_ref[0] = _dot(s, w_ref[0]) + b_ref[0]


def _modulation(c_all, mod_w, mod_b, tn=1152):
    depth, d, n = mod_w.shape
    r = c_all.shape[0]
    return pl.pallas_call(
        _mod_kernel,
        grid=(depth, n // tn),
        in_specs=[pl.BlockSpec((r, d), lambda l, j: (0, 0)),
                  pl.BlockSpec((1, d, tn), lambda l, j: (l, 0, j)),
                  pl.BlockSpec((1, 1, tn), lambda l, j: (l, 0, j))],
        out_specs=pl.BlockSpec((1, r, tn), lambda l, j: (l, 0, j)),
        out_shape=jax.ShapeDtypeStruct((depth, r, n), F32),
        compiler_params=_params(2),
        name="modulation",
    )(c_all, mod_w.astype(BF16), mod_b.reshape(depth, 1, n))


def _mod_row(mod_ref, idx, per_row):
    if per_row:
        return mod_ref[idx]
    return mod_ref[0, idx:idx + 1, :]


def _adaln(x, g, shift, scale, eps):
    y = x * lax.rsqrt(jnp.mean(x * x, axis=-1, keepdims=True) + eps)
    return (y * g) * (1.0 + scale) + shift


def _swiglu(h, wa_ref, wb_ref, w2_ref, n_chunks):
    d_ff = wa_ref.shape[1]
    cw = d_ff // n_chunks
    acc = None
    for c in range(n_chunks):
        a = _dot(h, wa_ref[:, c * cw:(c + 1) * cw])
        b = _dot(h, wb_ref[:, c * cw:(c + 1) * cw])
        t = ((a * jax.nn.sigmoid(a)) * b).astype(BF16)
        part = _dot(t, w2_ref[c * cw:(c + 1) * cw, :])
        acc = part if acc is None else acc + part
    return acc


def _ffn_kernel(*refs, merge, per_row, sub, n_chunks, eps):
    if merge:
        (x_ref, mod_ref, g_ref, ya_ref, gb_ref, ob_ref, pb_ref, wo_ref,
         wa_ref, wb_ref, w2_ref, o_ref) = refs
    else:
        x_ref, mod_ref, g_ref, wa_ref, wb_ref, w2_ref, o_ref = refs
    x = x_ref[...]
    if merge:
        y = ya_ref[...] + gb_ref[...] * _dot(ob_ref[...], pb_ref[...])
        x = x + _mod_row(mod_ref, 5, per_row) * _dot(y.astype(BF16), wo_ref[...])
    shift = _mod_row(mod_ref, 3 * sub, per_row)
    scale = _mod_row(mod_ref, 3 * sub + 1, per_row)
    gate = _mod_row(mod_ref, 3 * sub + 2, per_row)
    h = _adaln(x, g_ref[...], shift, scale, eps).astype(BF16)
    o_ref[...] = x + (0.5 * gate) * _swiglu(h, wa_ref, wb_ref, w2_ref, n_chunks)


def _ffn(x, mod, g, wa, wb, w2, cfg, *, sub, tm, rows_per_mod, merge_args=None):
    n, d = x.shape
    per_row = rows_per_mod is None
    merge = merge_args is not None
    row = lambda i: (i, 0)
    if per_row:
        mod_spec = pl.BlockSpec((mod.shape[0], tm, d), lambda i: (0, i, 0))
    else:
        tiles_per_mod = rows_per_mod // tm
        mod_spec = pl.BlockSpec((1, mod.shape[1], d), lambda i: (i // tiles_per_mod, 0, 0))
    in_specs = [pl.BlockSpec((tm, d), row), mod_spec, _const_spec((1, d))]
    args = [x, mod, g.reshape(1, d)]
    if merge:
        ya, gb, ob, pb, wo = merge_args
        in_specs += [pl.BlockSpec((tm, d), row), pl.BlockSpec((tm, d), row),
                     pl.BlockSpec((tm, ob.shape[1]), row), _const_spec(pb.shape), _const_spec(wo.shape)]
        args += [ya, gb, ob, pb, wo]
    in_specs += [_const_spec(wa.shape), _const_spec(wb.shape), _const_spec(w2.shape)]
    args += [wa, wb, w2]
    kern = functools.partial(_ffn_kernel, merge=merge, per_row=per_row, sub=sub, n_chunks=2, eps=cfg.eps)
    return pl.pallas_call(
        kern,
        grid=(n // tm,),
        in_specs=in_specs,
        out_specs=pl.BlockSpec((tm, d), row),
        out_shape=jax.ShapeDtypeStruct((n, d), F32),
        compiler_params=_params(1),
        name="merge_ffn" if merge else "ffn",
    )(*args)


def _gelu_tanh(x):
    return 0.5 * x * (1.0 + jnp.tanh(np.sqrt(2.0 / np.pi).astype(np.float32) * (x + 0.044715 * (x * x * x))))


def _head_mean(sq, bd_ref, head_dim):
    hi = sq.astype(BF16)
    lo = (sq - hi.astype(F32)).astype(BF16)
    return (_dot(hi, bd_ref[...]) + _dot(lo, bd_ref[...])) * (1.0 / head_dim)


def _rope(x, cos, sin_up, sin_dn, shift):
    parts = []
    for j in range(x.shape[1] // LANES):
        xs = x[:, j * LANES:(j + 1) * LANES]
        parts.append(xs * cos + pltpu.roll(xs, shift, 1) * sin_up + pltpu.roll(xs, LANES - shift, 1) * sin_dn)
    return parts[0] if len(parts) == 1 else jnp.concatenate(parts, axis=1)


def _proj_kernel(*refs, per_row, sample, cfg):
    (x_ref, mod_ref, g_ref, w1_ref, w2_ref, w3_ref, sg_ref, wm_ref, sb_ref, qg_ref, kg_ref,
     bdq_ref, bdk_ref, cos_ref, sup_ref, sdn_ref, pa_ref) = refs[:17]
    outs = refs[17:]
    eps, hd, d_a = cfg.eps, cfg.head_dim, cfg.d_a
    nq, nkv, nqi = cfg.n_heads * hd, cfg.n_kv * hd, cfg.n_idx_heads * cfg.idx_dim
    tm = x_ref.shape[0]

    x = x_ref[...]
    h = _adaln(x, g_ref[...], _mod_row(mod_ref, 3, per_row), _mod_row(mod_ref, 4, per_row), eps).astype(BF16)
    z1 = _dot(h, w1_ref[...])
    z2 = _dot(h, w2_ref[...])
    z3 = _dot(h, w3_ref[...])
    o = 0
    zu = z1[:, o:o + d_a]; o += d_a
    zv = z1[:, o:o + d_a]; o += d_a
    zq = z1[:, o:o + nq]; o += nq
    zk = z1[:, o:o + nkv]; o += nkv
    zva = z1[:, o:o + nkv]; o += nkv
    zqi = z1[:, o:o + nqi]

    u = _gelu_tanh(zu)
    gv = _gelu_tanh(zv)
    v = gv * lax.rsqrt(jnp.mean(gv * gv, axis=-1, keepdims=True) + eps) * sg_ref[...]
    vb = v.astype(BF16)
    ck = wm_ref.shape[1]
    rr = lax.broadcasted_iota(I32, (ck, ck), 0)
    cc = lax.broadcasted_iota(I32, (ck, ck), 1)
    lane_group = lax.broadcasted_iota(I32, (ck, d_a), 1) // (d_a // cfg.g_a)
    wms = [jnp.where(rr >= cc, wm_ref[g], 0.0).astype(BF16) for g in range(cfg.g_a)]
    mixed_chunks = []
    for c in range(tm // ck):
        vc = vb[c * ck:(c + 1) * ck, :]
        mixed = sb_ref[...]
        for g in range(cfg.g_a):
            mixed = mixed + jnp.where(lane_group == g, _dot(wms[g], vc), 0.0)
        mixed_chunks.append(mixed)
    mixed = mixed_chunks[0] if len(mixed_chunks) == 1 else jnp.concatenate(mixed_chunks, axis=0)
    o_a = (u * mixed).astype(BF16)
    ga = jax.nn.sigmoid(z3[:, :cfg.d_model])
    gb = jax.nn.sigmoid(z3[:, cfg.d_model:])
    ya = ga * _dot(o_a, pa_ref[...])

    cos, sup, sdn = cos_ref[...], sup_ref[...], sdn_ref[...]
    half = hd // 8
    q = zq * lax.rsqrt(_head_mean(zq * zq, bdq_ref, hd) + eps) * qg_ref[...]
    q = _rope(q, cos, sup, sdn, half)
    k = zk * lax.rsqrt(_head_mean(zk * zk, bdk_ref, hd) + eps) * kg_ref[...]
    k = _rope(k, cos, sup, sdn, half)
    qi = _rope(zqi, cos, sup, sdn, half)
    ki = _rope(z2, cos, sup, sdn, half)[:, :cfg.idx_dim]
    wi = z2[:, cfg.idx_dim:cfg.idx_dim + cfg.n_idx_heads] * (cfg.n_idx_heads ** -0.5)

    if sample:
        (k_ref, va_ref, ki_ref, v_ref, q_ref, qi_ref, wi_ref, ya_ref, gb_ref) = outs
        v_ref[...] = v
        q_ref[...] = q
        qi_ref[...] = qi
    else:
        (k_ref, va_ref, ki_ref, qh_ref, qih_ref, kh_ref, vh_ref, kib_ref, wi_ref, ya_ref, gb_ref) = outs
        qs = q * (hd ** -0.5)
        for hh in range(cfg.n_heads):
            qh_ref[0, hh] = qs[:, hh * hd:(hh + 1) * hd].astype(BF16)
        for hh in range(cfg.n_idx_heads):
            qih_ref[0, hh] = qi[:, hh * cfg.idx_dim:(hh + 1) * cfg.idx_dim].astype(BF16)
        for hh in range(cfg.n_kv):
            kh_ref[0, hh] = k[:, hh * hd:(hh + 1) * hd].astype(BF16)
            vh_ref[0, hh] = zva[:, hh * hd:(hh + 1) * hd].astype(BF16)
        kib_ref[0] = ki.astype(BF16)
    k_ref[...] = k
    va_ref[...] = zva
    ki_ref[...] = ki
    wi_ref[...] = wi
    ya_ref[...] = ya
    gb_ref[...] = gb


def _rope_tables(pos, cfg):
    hd = cfg.head_dim
    rd = hd // 4
    half = rd // 2
    freqs = cfg.rope_theta ** (-jnp.arange(half, dtype=F32) * 2.0 / rd)
    ang = pos.astype(F32)[:, None] * freqs[None, :]
    cos, sin = jnp.cos(ang), jnp.sin(ang)
    n = pos.shape[0]
    one = jnp.ones((n, hd - rd), F32)
    zero = jnp.zeros((n, hd - rd), F32)
    zh = jnp.zeros((n, half), F32)
    cos_h = jnp.concatenate([cos, cos, one], axis=1)
    sup_h = jnp.concatenate([zh, sin, zero], axis=1)
    sdn_h = jnp.concatenate([-sin, zh, zero], axis=1)
    rep = LANES // hd
    return tuple(jnp.tile(t, (1, rep)) for t in (cos_h, sup_h, sdn_h))


def _proj_weights(lw, cfg):
    d_a, hd = cfg.d_a, cfg.head_dim
    nq, nkv, nqi = cfg.n_heads * hd, cfg.n_kv * hd, cfg.n_idx_heads * cfg.idx_dim
    n1 = 2 * d_a + nq + 2 * nkv + nqi
    n2 = cfg.idx_dim + cfg.n_idx_heads
    w_in = lw["w_in"]
    w1 = w_in[:, :n1].astype(BF16)
    w2 = jnp.pad(w_in[:, n1:n1 + n2], ((0, 0), (0, LANES - n2))).astype(BF16)
    w3 = w_in[:, n1 + n2:].astype(BF16)
    gd = d_a // cfg.g_a

    def block_diag(width):
        hidx = jnp.arange(width) // hd
        return (hidx[:, None] == hidx[None, :]).astype(BF16)

    return dict(
        w1=w1, w2=w2, w3=w3,
        sg=lw["sgu_norm_g"].reshape(1, d_a),
        qg=jnp.tile(lw["q_norm_g"], cfg.n_heads).reshape(1, nq),
        kg=jnp.tile(lw["k_norm_g"], cfg.n_kv).reshape(1, nkv),
        bdq=block_diag(nq), bdk=block_diag(nkv),
        pa=lw["w_branch_a"].astype(BF16),
        wm_p=lw["sgu_w"],
        sb_p=jnp.repeat(lw["sgu_b"].T, gd, axis=1),
    )


def _sample_gate_weights(lw, cfg, n_seq):
    t = cfg.dec_seq
    gd = cfg.d_a // cfg.g_a
    eye = jnp.eye(n_seq, dtype=F32)
    wm = jax.vmap(lambda w: jnp.kron(eye, w))(lw["sgu_w"][:, :t, :t])
    sb = jnp.tile(jnp.repeat(lw["sgu_b"][:, :t].T, gd, axis=1), (n_seq, 1))
    return wm, sb


def _proj(x, mod, g, pw, wm, sb, tabs, cfg, *, tm, rows_per_mod, sample, n_batch=None):
    n, d = x.shape
    per_row = rows_per_mod is None
    row = lambda i: (i, 0)
    if per_row:
        mod_spec = pl.BlockSpec((mod.shape[0], tm, d), lambda i: (0, i, 0))
        tab_spec = pl.BlockSpec((tm, LANES), row)
    else:
        tiles_per_mod = rows_per_mod // tm
        mod_spec = pl.BlockSpec((1, mod.shape[1], d), lambda i: (i // tiles_per_mod, 0, 0))
        tab_spec = pl.BlockSpec((tm, LANES), lambda i: (i % tiles_per_mod, 0))
    consts = [g.reshape(1, d), pw["w1"], pw["w2"], pw["w3"], pw["sg"], wm, sb, pw["qg"], pw["kg"],
              pw["bdq"], pw["bdk"]]
    in_specs = ([pl.BlockSpec((tm, d), row), mod_spec] + [_const_spec(a.shape) for a in consts]
                + [tab_spec] * 3 + [_const_spec(pw["pa"].shape)])
    args = [x, mod] + consts + list(tabs) + [pw["pa"]]
    hd = cfg.head_dim
    nq, nkv, nqi = cfg.n_heads * hd, cfg.n_kv * hd, cfg.n_idx_heads * cfg.idx_dim

    def rows(width, dtype=F32):
        return pl.BlockSpec((tm, width), row), jax.ShapeDtypeStruct((n, width), dtype)

    outs = [rows(nkv), rows(nkv), rows(cfg.idx_dim)]
    if sample:
        outs += [rows(cfg.d_a), rows(nq), rows(nqi)]
    else:
        s = rows_per_mod
        tps = s // tm

        def heads(nh, width):
            return (pl.BlockSpec((1, nh, tm, width), lambda i: (i // tps, 0, i % tps, 0)),
                    jax.ShapeDtypeStruct((n_batch, nh, s, width), BF16))

        outs += [heads(cfg.n_heads, hd), heads(cfg.n_idx_heads, cfg.idx_dim), heads(cfg.n_kv, hd),
                 heads(cfg.n_kv, hd),
                 (pl.BlockSpec((1, tm, cfg.idx_dim), lambda i: (i // tps, i % tps, 0)),
                  jax.ShapeDtypeStruct((n_batch, s, cfg.idx_dim), BF16))]
    outs += [rows(cfg.n_idx_heads), rows(d), rows(d)]
    kern = functools.partial(_proj_kernel, per_row=per_row, sample=sample, cfg=cfg)
    return pl.pallas_call(
        kern,
        grid=(n // tm,),
        in_specs=in_specs,
        out_specs=[o[0] for o in outs],
        out_shape=[o[1] for o in outs],
        compiler_params=_params(1),
        name="proj_sample" if sample else "proj_prompt",
    )(*args)


def _score_key(x):
    b = lax.bitcast_convert_type(x + 0.0, I32)
    return b ^ ((b >> 31) & 0x7FFFFFFF)


def _count(sc_ref, n_tiles, tk, pred):
    rows = sc_ref.shape[0]

    def body(j, acc):
        off = pl.multiple_of(j * tk, tk)
        x = sc_ref[:, pl.ds(off, tk)]
        for c in range(tk // LANES):
            acc = acc + jnp.where(pred(x[:, c * LANES:(c + 1) * LANES], off + c * LANES), 1, 0)
        return acc

    acc = lax.fori_loop(0, n_tiles, body, jnp.zeros((rows, LANES), I32))
    return jnp.sum(acc.astype(F32), axis=1, keepdims=True).astype(I32)


def _select_threshold(sc_ref, n_tiles, tk, k_eff, idx_bits):
    rows = sc_ref.shape[0]
    bcast = lambda a: jnp.broadcast_to(a, (rows, LANES))
    lane = lax.broadcasted_iota(I32, (rows, LANES), 1)

    def count_ge(cand):
        cb = bcast(cand)
        return _count(sc_ref, n_tiles, tk, lambda x, _: x >= cb)

    t0 = jnp.where(count_ge(jnp.zeros((rows, 1), I32)) >= k_eff, 0, INT_MIN).astype(I32)

    def value_step(it, t):
        cand = t + lax.shift_left(jnp.int32(1), 30 - it)
        return jnp.where(count_ge(cand) >= k_eff, cand, t)

    t = lax.fori_loop(0, 31, value_step, t0)
    tb = bcast(t)
    need = k_eff - _count(sc_ref, n_tiles, tk, lambda x, _: x > tb)

    def index_step(it, lo):
        cand = lo + lax.shift_left(jnp.int32(1), idx_bits - 1 - it)
        cb = bcast(cand)
        big = jnp.int32(2 ** 30)
        f = _count(sc_ref, n_tiles, tk, lambda x, off: jnp.where(x == tb, lane + off, big) < cb)
        return jnp.where(f < need, cand, lo)

    j = lax.fori_loop(0, idx_bits, index_step, jnp.zeros((rows, 1), I32))
    return t, j


def _selection_bias(x, col, tb, jb):
    return jnp.where(x > tb, 0.0, jnp.where(x < tb, NEG_BIG, jnp.where(col <= jb, 0.0, NEG_BIG)))


def _softmax_step(s, v, m_ref, l_ref, acc_ref, idx):
    m_old = m_ref[idx]
    m_new = jnp.maximum(m_old, jnp.max(s, axis=1, keepdims=True))
    p = jnp.exp(s - m_new)
    alpha = jnp.exp(m_old - m_new)
    l_ref[idx] = alpha * l_ref[idx] + jnp.sum(p, axis=1, keepdims=True)
    acc_ref[idx] = alpha * acc_ref[idx] + _dot(p.astype(BF16), v)
    m_ref[idx] = m_new


def _prompt_attn_kernel(qi_ref, wi_ref, ki_ref, q_ref, k_ref, v_ref, o_ref,
                        sc_ref, m_ref, l_ref, acc_ref, *, qb, tk, topk, idx_bits, cfg):
    i = pl.program_id(1)
    n_tiles = ((i + 1) * qb + tk - 1) // tk
    row_t = i * qb + lax.broadcasted_iota(I32, (qb, 1), 0)
    lane = lax.broadcasted_iota(I32, (qb, tk), 1)
    wi = wi_ref[0]
    rep = cfg.n_heads // cfg.n_kv

    def score_tile(j, carry):
        off = pl.multiple_of(j * tk, tk)
        kib = ki_ref[0, pl.ds(off, tk), :]
        acc = jnp.zeros((qb, tk), F32)
        for h in range(cfg.n_idx_heads):
            acc = acc + wi[:, h:h + 1] * jnp.maximum(_dot_nt(qi_ref[0, h], kib), 0.0)
        sc_ref[:, pl.ds(off, tk)] = jnp.where(lane + off <= row_t, _score_key(acc), INT_MIN)
        return carry

    lax.fori_loop(0, n_tiles, score_tile, 0)

    k_eff = jnp.minimum(row_t + 1, topk)
    t, jcol = _select_threshold(sc_ref, n_tiles, tk, k_eff, idx_bits)
    tb = jnp.broadcast_to(t, (qb, tk))
    jb = jnp.broadcast_to(jcol, (qb, tk))

    m_ref[...] = jnp.full(m_ref.shape, NEG_BIG, F32)
    l_ref[...] = jnp.zeros(l_ref.shape, F32)
    acc_ref[...] = jnp.zeros(acc_ref.shape, F32)

    def attend_tile(j, carry):
        off = pl.multiple_of(j * tk, tk)
        bias = _selection_bias(sc_ref[:, pl.ds(off, tk)], lane + off, tb, jb)
        bias = jnp.concatenate([bias] * rep, axis=0)
        for g in range(cfg.n_kv):
            q2 = jnp.concatenate([q_ref[0, g * rep + r] for r in range(rep)], axis=0)
            s = _dot_nt(q2, k_ref[0, g, pl.ds(off, tk), :]) + bias
            _softmax_step(s, v_ref[0, g, pl.ds(off, tk), :], m_ref, l_ref, acc_ref, g)
        return carry

    lax.fori_loop(0, n_tiles, attend_tile, 0)

    heads = []
    for h in range(cfg.n_heads):
        g, r = divmod(h, rep)
        heads.append(acc_ref[g, r * qb:(r + 1) * qb, :] / l_ref[g, r * qb:(r + 1) * qb, :])
    o_ref[0] = jnp.concatenate(heads, axis=1).astype(BF16)


def _prompt_attention(qih, wi, kib, qh, kh, vh, cfg, *, qb=128, tk=256):
    nb, _, s, hd = qh.shape
    topk = min(cfg.topk_max, s // 4)
    rep = cfg.n_heads // cfg.n_kv
    kern = functools.partial(_prompt_attn_kernel, qb=qb, tk=tk, topk=topk,
                             idx_bits=max(1, int(np.ceil(np.log2(s)))), cfg=cfg)
    per_batch = lambda b, i: (b, 0, 0, 0)
    return pl.pallas_call(
        kern,
        grid=(nb, s // qb),
        in_specs=[pl.BlockSpec((1, cfg.n_idx_heads, qb, cfg.idx_dim), lambda b, i: (b, 0, i, 0)),
                  pl.BlockSpec((1, qb, cfg.n_idx_heads), lambda b, i: (b, i, 0)),
                  pl.BlockSpec((1, s, cfg.idx_dim), lambda b, i: (b, 0, 0)),
                  pl.BlockSpec((1, cfg.n_heads, qb, hd), lambda b, i: (b, 0, i, 0)),
                  pl.BlockSpec((1, cfg.n_kv, s, hd), per_batch),
                  pl.BlockSpec((1, cfg.n_kv, s, hd), per_batch)],
        out_specs=pl.BlockSpec((1, qb, cfg.n_heads * hd), lambda b, i: (b, i, 0)),
        out_shape=jax.ShapeDtypeStruct((nb, s, cfg.n_heads * hd), BF16),
        scratch_shapes=[pltpu.VMEM((qb, s), I32),
                        pltpu.VMEM((cfg.n_kv, rep * qb, 1), F32),
                        pltpu.VMEM((cfg.n_kv, rep * qb, 1), F32),
                        pltpu.VMEM((cfg.n_kv, rep * qb, hd), F32)],
        compiler_params=_params(2),
        name="prompt_attention",
    )(qih, wi, kib, qh, kh, vh)


TOK_PAD = 8
PAGES_PER_STEP = 16


def _samp_index_kernel(pt_ref, qi_ref, w_ref, kit_ref, *refs, n_pages_step, tk, topk, idx_bits, cfg):
    page_refs = refs[:n_pages_step]
    bias_ref, sc_ref = refs[n_pages_step:]
    c = pl.program_id(1)
    n_chunks = pl.num_programs(1)
    nh = cfg.n_idx_heads
    qi = qi_ref[0]
    w = w_ref[0]
    width = n_pages_step * cfg.page_size
    total = sc_ref.shape[1]

    def token_scores(keys_bf16):
        y = w * jnp.maximum(_dot_nt(qi, keys_bf16), 0.0)
        return jnp.sum(y.reshape(nh, TOK_PAD, y.shape[1]), axis=0)

    kc = jnp.concatenate([r[...] for r in page_refs], axis=0).astype(BF16)
    sc_ref[:, pl.ds(pl.multiple_of(c * width, width), width)] = _score_key(token_scores(kc))

    @pl.when(c == n_chunks - 1)
    def _():
        tail = kit_ref.shape[1]
        tok = lax.broadcasted_iota(I32, (TOK_PAD, tail), 0)
        new = lax.broadcasted_iota(I32, (TOK_PAD, tail), 1)
        valid = new <= jnp.minimum(tok, cfg.dec_seq - 1)
        sc_ref[:, total - tail:] = jnp.where(valid, _score_key(token_scores(kit_ref[0])), INT_MIN)
        tok1 = lax.broadcasted_iota(I32, (TOK_PAD, 1), 0)
        k_eff = jnp.minimum(cfg.past_len + jnp.minimum(tok1, cfg.dec_seq - 1) + 1, topk)
        t, jcol = _select_threshold(sc_ref, total // tk, tk, k_eff, idx_bits)
        tb = jnp.broadcast_to(t, (TOK_PAD, tk))
        jb = jnp.broadcast_to(jcol, (TOK_PAD, tk))
        lane = lax.broadcasted_iota(I32, (TOK_PAD, tk), 1)
        for jt in range(total // tk):
            bias_ref[0, :, jt * tk:(jt + 1) * tk] = _selection_bias(
                sc_ref[:, jt * tk:(jt + 1) * tk], lane + jt * tk, tb, jb)


def _samp_attn_kernel(pt_ref, q_ref, bias_ref, biast_ref, kt_ref, vt_ref, *refs, n_pages_step, cfg):
    k_pages = refs[:n_pages_step]
    v_pages = refs[n_pages_step:2 * n_pages_step]
    o_ref, m_ref, l_ref, acc_ref = refs[2 * n_pages_step:]
    c = pl.program_id(1)
    n_chunks = pl.num_programs(1)
    q = q_ref[0]
    hd = cfg.head_dim
    rep = cfg.n_heads // cfg.n_kv

    @pl.when(c == 0)
    def _():
        m_ref[...] = jnp.full(m_ref.shape, NEG_BIG, F32)
        l_ref[...] = jnp.zeros(l_ref.shape, F32)
        acc_ref[...] = jnp.zeros(acc_ref.shape, F32)

    def step(kk, vv, bias):
        s = _dot_nt(q, kk) + jnp.concatenate([bias] * cfg.n_heads, axis=0)
        _softmax_step(s, vv, m_ref, l_ref, acc_ref, 0)

    kc = jnp.concatenate([r[...] for r in k_pages], axis=0).astype(BF16)
    vc = jnp.concatenate([r[...] for r in v_pages], axis=0).astype(BF16)
    step(kc, vc, bias_ref[0])

    @pl.when(c == n_chunks - 1)
    def _():
        step(kt_ref[0], vt_ref[0], biast_ref[0])
        o = acc_ref[0] / l_ref[0]
        for h in range(cfg.n_heads):
            g = h // rep
            o_ref[0, h] = o[h * TOK_PAD:(h + 1) * TOK_PAD, g * hd:(g + 1) * hd]


def _sample_attention(l, q, qi, wi, k_new, v_new, ki_new, cache_k, cache_v, cache_idx_k, page_table, cfg):
    db, n_pages = page_table.shape
    t, hd, ps = cfg.dec_seq, cfg.head_dim, cfg.page_size
    nh, nkv, nih, di = cfg.n_heads, cfg.n_kv, cfg.n_idx_heads, cfg.idx_dim
    rep = nh // nkv
    pstep = min(PAGES_PER_STEP, n_pages)
    n_chunks = n_pages // pstep
    width = pstep * ps
    tail = LANES
    total = cfg.past_len + tail
    topk = min(cfg.topk_max, (cfg.past_len + t) // 4)
    idx_bits = int(np.ceil(np.log2(total)))
    tk = 256 if total % 256 == 0 else LANES

    def head_major(a, nheads, dim):
        a = a.reshape(db, t, nheads, dim).transpose(0, 2, 1, 3)
        return jnp.pad(a, ((0, 0), (0, 0), (0, TOK_PAD - t), (0, 0)))

    qi_rows = head_major(qi, nih, di).reshape(db, nih * TOK_PAD, di).astype(BF16)
    w_rows = head_major(wi, nih, 1).reshape(db, nih * TOK_PAD, 1)
    group = (jnp.arange(nh) // rep)[:, None] == jnp.arange(nkv)[None, :]
    q_rows = head_major(q * (hd ** -0.5), nh, hd)[:, :, :, None, :] * group[None, :, None, :, None].astype(F32)
    q_rows = q_rows.reshape(db, nh * TOK_PAD, nkv * hd).astype(BF16)

    def tail_pad(a):
        return jnp.pad(a.reshape(db, t, a.shape[-1]), ((0, 0), (0, tail - t), (0, 0))).astype(BF16)

    ki_tail, k_tail, v_tail = tail_pad(ki_new), tail_pad(k_new), tail_pad(v_new)
    ck = cache_k.reshape(cache_k.shape[:3] + (nkv * hd,))
    cv = cache_v.reshape(cache_v.shape[:3] + (nkv * hd,))

    def page_spec(j, w):
        return pl.BlockSpec((None, None, ps, w), lambda b, c, pt: (l, pt[b, c * pstep + j], 0, 0))

    per_seq3 = lambda b, c, pt: (b, 0, 0)
    bias = pl.pallas_call(
        functools.partial(_samp_index_kernel, n_pages_step=pstep, tk=tk, topk=topk, idx_bits=idx_bits, cfg=cfg),
        grid_spec=pltpu.PrefetchScalarGridSpec(
            num_scalar_prefetch=1,
            grid=(db, n_chunks),
            in_specs=[pl.BlockSpec((1, nih * TOK_PAD, di), per_seq3),
                      pl.BlockSpec((1, nih * TOK_PAD, 1), per_seq3),
                      pl.BlockSpec((1, tail, di), per_seq3)]
                     + [page_spec(j, di) for j in range(pstep)],
            out_specs=pl.BlockSpec((1, TOK_PAD, total), per_seq3),
            scratch_shapes=[pltpu.VMEM((TOK_PAD, total), I32)]),
        out_shape=jax.ShapeDtypeStruct((db, TOK_PAD, total), F32),
        compiler_params=_params(2),
        name="sample_index",
    )(page_table, qi_rows, w_rows, ki_tail, *([cache_idx_k] * pstep))

    out = pl.pallas_call(
        functools.partial(_samp_attn_kernel, n_pages_step=pstep, cfg=cfg),
        grid_spec=pltpu.PrefetchScalarGridSpec(
            num_scalar_prefetch=1,
            grid=(db, n_chunks),
            in_specs=[pl.BlockSpec((1, nh * TOK_PAD, nkv * hd), per_seq3),
                      pl.BlockSpec((1, TOK_PAD, width), lambda b, c, pt: (b, 0, c)),
                      pl.BlockSpec((1, TOK_PAD, tail), lambda b, c, pt: (b, 0, total // tail - 1)),
                      pl.BlockSpec((1, tail, nkv * hd), per_seq3),
                      pl.BlockSpec((1, tail, nkv * hd), per_seq3)]
                     + [page_spec(j, nkv * hd) for j in range(pstep)] * 2,
            out_specs=pl.BlockSpec((1, nh, TOK_PAD, hd), lambda b, c, pt: (b, 0, 0, 0)),
            scratch_shapes=[pltpu.VMEM((1, nh * TOK_PAD, 1), F32),
                            pltpu.VMEM((1, nh * TOK_PAD, 1), F32),
                            pltpu.VMEM((1, nh * TOK_PAD, nkv * hd), F32)]),
        out_shape=jax.ShapeDtypeStruct((db, nh, TOK_PAD, hd), F32),
        compiler_params=_params(2),
        name="sample_attention",
    )(page_table, q_rows, bias, bias, k_tail, v_tail, *([ck] * pstep), *([cv] * pstep))
    return out[:, :, :t, :].transpose(0, 2, 1, 3).reshape(db * t, nh * hd).astype(BF16)


def _forward(cfg, x_prompt, x_sample, c_prompt, c_sample, cache_k, cache_v, cache_idx_k, page_table,
             mod_w, mod_b, norm_g, ffn_w_in, ffn_w_out, w_in, sgu_norm_g, sgu_w, sgu_b,
             q_norm_g, k_norm_g, w_branch_a, w_branch_b, w_out):
    nb, s, d = x_prompt.shape
    db, t, _ = x_sample.shape
    depth = mod_w.shape[0]
    hd = cfg.head_dim
    n_mod = cfg.n_sub * 3

    n_seq = nb + db
    r_pad = -(-n_seq // 8) * 8
    c_all = jnp.pad(jnp.concatenate([c_prompt, c_sample], axis=0), ((0, r_pad - n_seq), (0, 0)))
    mod = _modulation(c_all, mod_w, mod_b).reshape(depth, r_pad, n_mod, d)
    mod_p = mod[:, :nb]
    mod_s = jnp.repeat(mod[:, nb:n_seq], t, axis=1).transpose(0, 2, 1, 3)

    tabs_p = _rope_tables(jnp.arange(s), cfg)
    tabs_s = _rope_tables(jnp.tile(cfg.past_len + jnp.arange(t), db), cfg)

    xp = x_prompt.reshape(nb * s, d)
    xs = x_sample.reshape(db * t, d)
    tm_ffn = min(512, s)
    tm_proj = min(256, s)
    d_ff = ffn_w_out.shape[2]
    leaves = [[] for _ in range(7)]
    for l in range(depth):
        lw = dict(w_in=w_in[l], sgu_norm_g=sgu_norm_g[l], sgu_w=sgu_w[l], sgu_b=sgu_b[l],
                  q_norm_g=q_norm_g[l], k_norm_g=k_norm_g[l], w_branch_a=w_branch_a[l])
        pw = _proj_weights(lw, cfg)
        wm_s, sb_s = _sample_gate_weights(lw, cfg, db)
        ffn_w = [(ffn_w_in[l, j, :, :d_ff].astype(BF16), ffn_w_in[l, j, :, d_ff:].astype(BF16),
                  ffn_w_out[l, j].astype(BF16)) for j in range(2)]
        pb = w_branch_b[l].astype(BF16)
        wo = w_out[l].astype(BF16)

        xp = _ffn(xp, mod_p[l], norm_g[l, 0], *ffn_w[0], cfg, sub=0, tm=tm_ffn, rows_per_mod=s)
        (k_p, va_p, ki_p, qh, qih, kh, vh, kib, wi_p, ya_p, gb_p) = _proj(
            xp, mod_p[l], norm_g[l, 1], pw, pw["wm_p"], pw["sb_p"], tabs_p, cfg,
            tm=tm_proj, rows_per_mod=s, sample=False, n_batch=nb)
        ob_p = _prompt_attention(qih, wi_p.reshape(nb, s, -1), kib, qh, kh, vh, cfg)
        xp = _ffn(xp, mod_p[l], norm_g[l, 2], *ffn_w[1], cfg, sub=2, tm=tm_ffn, rows_per_mod=s,
                  merge_args=(ya_p, gb_p, ob_p.reshape(nb * s, -1), pb, wo))

        xs = _ffn(xs, mod_s[l], norm_g[l, 0], *ffn_w[0], cfg, sub=0, tm=db * t, rows_per_mod=None)
        (k_s, va_s, ki_s, v_s, q_s, qi_s, wi_s, ya_s, gb_s) = _proj(
            xs, mod_s[l], norm_g[l, 1], pw, wm_s, sb_s, tabs_s, cfg,
            tm=db * t, rows_per_mod=None, sample=True)
        ob_s = _sample_attention(l, q_s, qi_s, wi_s, k_s, va_s, ki_s, cache_k, cache_v, cache_idx_k,
                                 page_table, cfg)
        xs = _ffn(xs, mod_s[l], norm_g[l, 2], *ffn_w[1], cfg, sub=2, tm=db * t, rows_per_mod=None,
                  merge_args=(ya_s, gb_s, ob_s, pb, wo))

        leaves[0].append(k_p.reshape(nb, s, cfg.n_kv, hd))
        leaves[1].append(va_p.reshape(nb, s, cfg.n_kv, hd))
        leaves[2].append(ki_p.reshape(nb, s, cfg.idx_dim))
        leaves[3].append(k_s.reshape(db, t, cfg.n_kv, hd))
        leaves[4].append(va_s.reshape(db, t, cfg.n_kv, hd))
        leaves[5].append(ki_s.reshape(db, t, cfg.idx_dim))
        leaves[6].append(v_s.reshape(db, t, cfg.d_a))
    return (xp.reshape(nb, s, d), xs.reshape(db, t, d)) + tuple(jnp.stack(v) for v in leaves)


def kernel(x_prompt, x_sample, c_prompt, c_sample, cache_k, cache_v, cache_idx_k, page_table, mod_w, mod_b, norm_g, ffn_w_in, ffn_w_out, w_in, sgu_norm_g, sgu_w, sgu_b, q_norm_g, k_norm_g, w_branch_a, w_branch_b, w_out):
    return _forward(CFG, x_prompt, x_sample, c_prompt, c_sample, cache_k, cache_v, cache_idx_k, page_table,
                    mod_w, mod_b, norm_g, ffn_w_in, ffn_w_out, w_in, sgu_norm_g, sgu_w, sgu_b,
                    q_norm_g, k_norm_g, w_branch_a, w_branch_b, w_out)
```

```python
import collections
import functools

import jax
import jax.numpy as jnp
import numpy as np
from jax import lax
from jax.experimental import pallas as pl
from jax.experimental.pallas import tpu as pltpu

F32 = jnp.float32
BF16 = jnp.bfloat16
I32 = jnp.int32

LANES = 128
INT_MIN = -(2 ** 31)
NEG_BIG = -1e30
LOG2_E = 1.4426950408889634
VMEM_LIMIT = 56 * 1024 * 1024

Config = collections.namedtuple(
    "Config",
    "d_model seq depth dec_seq past_len page_size chunk d_a g_a n_heads head_dim n_kv "
    "n_idx_heads idx_dim topk_max rope_theta d_ff n_sub eps")

CFG = Config(d_model=1024, seq=8192, depth=2, dec_seq=4, past_len=16384, page_size=128, chunk=128,
             d_a=512, g_a=8, n_heads=8, head_dim=64, n_kv=4, n_idx_heads=8, idx_dim=64,
             topk_max=256, rope_theta=500000.0, d_ff=2816, n_sub=3, eps=1e-6)


def _dot(a, b):
    return lax.dot_general(a, b, (((1,), (0,)), ((), ())), preferred_element_type=F32)


def _dot_nt(a, b):
    return lax.dot_general(a, b, (((1,), (1,)), ((), ())), preferred_element_type=F32)


def _const_spec(shape):
    nd = len(shape)
    return pl.BlockSpec(shape, lambda *_: (0,) * nd, pipeline_mode=pl.Buffered(1))


def _params(n_grid):
    return pltpu.CompilerParams(dimension_semantics=("arbitrary",) * n_grid,
                                vmem_limit_bytes=VMEM_LIMIT)


def _mod_kernel(c_ref, w_ref, b_ref, o_ref):
    c = c_ref[...]
    s = (c * jax.nn.sigmoid(c)).astype(BF16)
    o_ref[0] = _dot(s, w_ref[0]) + b_ref[0]


def _modulation(c_all, mod_w, mod_b, tn=1152):
    depth, d, n = mod_w.shape
    r = c_all.shape[0]
    return pl.pallas_call(
        _mod_kernel,
        grid=(depth, n // tn),
        in_specs=[pl.BlockSpec((r, d), lambda l, j: (0, 0)),
                  pl.BlockSpec((1, d, tn), lambda l, j: (l, 0, j)),
                  pl.BlockSpec((1, 1, tn), lambda l, j: (l, 0, j))],
        out_specs=pl.BlockSpec((1, r, tn), lambda l, j: (l, 0, j)),
        out_shape=jax.ShapeDtypeStruct((depth, r, n), F32),
        compiler_params=_params(2),
        name="modulation",
    )(c_all, mod_w.astype(BF16), mod_b.reshape(depth, 1, n))


def _mod_row(mod_ref, idx, per_row):
    if per_row:
        return mod_ref[idx]
    return mod_ref[0, idx:idx + 1, :]


def _adaln(x, g, shift, scale, eps):
    y = x * lax.rsqrt(jnp.mean(x * x, axis=-1, keepdims=True) + eps)
    return (y * g) * (1.0 + scale) + shift


def _swiglu(h, wa_ref, wb_ref, w2_ref, n_chunks):
    d_ff = wa_ref.shape[1]
    cw = d_ff // n_chunks
    acc = None
    for c in range(n_chunks):
        a = _dot(h, wa_ref[:, c * cw:(c + 1) * cw])
        b = _dot(h, wb_ref[:, c * cw:(c + 1) * cw])
        t = ((a * jax.nn.sigmoid(a)) * b).astype(BF16)
        part = _dot(t, w2_ref[c * cw:(c + 1) * cw, :])
        acc = part if acc is None else acc + part
    return acc


def _ffn_kernel(*refs, merge, per_row, sub, n_chunks, eps):
    if merge:
        (x_ref, mod_ref, g_ref, ya_ref, gb_ref, ob_ref, pb_ref, wo_ref,
         wa_ref, wb_ref, w2_ref, o_ref) = refs
    else:
        x_ref, mod_ref, g_ref, wa_ref, wb_ref, w2_ref, o_ref = refs
    x = x_ref[...]
    if merge:
        y = ya_ref[...] + gb_ref[...] * _dot(ob_ref[...], pb_ref[...])
        x = x + _mod_row(mod_ref, 5, per_row) * _dot(y.astype(BF16), wo_ref[...])
    shift = _mod_row(mod_ref, 3 * sub, per_row)
    scale = _mod_row(mod_ref, 3 * sub + 1, per_row)
    gate = _mod_row(mod_ref, 3 * sub + 2, per_row)
    h = _adaln(x, g_ref[...], shift, scale, eps).astype(BF16)
    o_ref[...] = x + (0.5 * gate) * _swiglu(h, wa_ref, wb_ref, w2_ref, n_chunks)


def _ffn(x, mod, g, wa, wb, w2, cfg, *, sub, tm, rows_per_mod, merge_args=None):
    n, d = x.shape
    per_row = rows_per_mod is None
    merge = merge_args is not None
    row = lambda i: (i, 0)
    if per_row:
        mod_spec = pl.BlockSpec((mod.shape[0], tm, d), lambda i: (0, i, 0))
    else:
        tiles_per_mod = rows_per_mod // tm
        mod_spec = pl.BlockSpec((1, mod.shape[1], d), lambda i: (i // tiles_per_mod, 0, 0))
    in_specs = [pl.BlockSpec((tm, d), row), mod_spec, _const_spec((1, d))]
    args = [x, mod, g.reshape(1, d)]
    if merge:
        ya, gb, ob, pb, wo = merge_args
        in_specs += [pl.BlockSpec((tm, d), row), pl.BlockSpec((tm, d), row),
                     pl.BlockSpec((tm, ob.shape[1]), row), _const_spec(pb.shape), _const_spec(wo.shape)]
        args += [ya, gb, ob, pb, wo]
    in_specs += [_const_spec(wa.shape), _const_spec(wb.shape), _const_spec(w2.shape)]
    args += [wa, wb, w2]
    kern = functools.partial(_ffn_kernel, merge=merge, per_row=per_row, sub=sub, n_chunks=2, eps=cfg.eps)
    return pl.pallas_call(
        kern,
        grid=(n // tm,),
        in_specs=in_specs,
        out_specs=pl.BlockSpec((tm, d), row),
        out_shape=jax.ShapeDtypeStruct((n, d), F32),
        compiler_params=_params(1),
        name="merge_ffn" if merge else "ffn",
    )(*args)


def _gelu_tanh(x):
    return 0.5 * x * (1.0 + jnp.tanh(np.sqrt(2.0 / np.pi).astype(np.float32) * (x + 0.044715 * (x * x * x))))


def _head_mean(sq, bd_ref, head_dim):
    hi = sq.astype(BF16)
    lo = (sq - hi.astype(F32)).astype(BF16)
    return (_dot(hi, bd_ref[...]) + _dot(lo, bd_ref[...])) * (1.0 / head_dim)


def _rope(x, cos, sin_up, sin_dn, shift):
    parts = []
    for j in range(x.shape[1] // LANES):
        xs = x[:, j * LANES:(j + 1) * LANES]
        parts.append(xs * cos + pltpu.roll(xs, shift, 1) * sin_up + pltpu.roll(xs, LANES - shift, 1) * sin_dn)
    return parts[0] if len(parts) == 1 else jnp.concatenate(parts, axis=1)


def _proj_kernel(*refs, per_row, sample, cfg):
    (x_ref, mod_ref, g_ref, w1_ref, w2_ref, w3_ref, sg_ref, wm_ref, sb_ref, qg_ref, kg_ref,
     bdq_ref, bdk_ref, cos_ref, sup_ref, sdn_ref, pa_ref) = refs[:17]
    outs = refs[17:]
    eps, hd, d_a = cfg.eps, cfg.head_dim, cfg.d_a
    nq, nkv, nqi = cfg.n_heads * hd, cfg.n_kv * hd, cfg.n_idx_heads * cfg.idx_dim
    tm = x_ref.shape[0]

    x = x_ref[...]
    h = _adaln(x, g_ref[...], _mod_row(mod_ref, 3, per_row), _mod_row(mod_ref, 4, per_row), eps).astype(BF16)
    z1 = _dot(h, w1_ref[...])
    z2 = _dot(h, w2_ref[...])
    z3 = _dot(h, w3_ref[...])
    o = 0
    zu = z1[:, o:o + d_a]; o += d_a
    zv = z1[:, o:o + d_a]; o += d_a
    zq = z1[:, o:o + nq]; o += nq
    zk = z1[:, o:o + nkv]; o += nkv
    zva = z1[:, o:o + nkv]; o += nkv
    zqi = z1[:, o:o + nqi]

    u = _gelu_tanh(zu)
    gv = _gelu_tanh(zv)
    v = gv * lax.rsqrt(jnp.mean(gv * gv, axis=-1, keepdims=True) + eps) * sg_ref[...]
    vb = v.astype(BF16)
    ck = wm_ref.shape[1]
    rr = lax.broadcasted_iota(I32, (ck, ck), 0)
    cc = lax.broadcasted_iota(I32, (ck, ck), 1)
    lane_group = lax.broadcasted_iota(I32, (ck, d_a), 1) // (d_a // cfg.g_a)
    wms = [jnp.where(rr >= cc, wm_ref[g], 0.0).astype(BF16) for g in range(cfg.g_a)]
    mixed_chunks = []
    for c in range(tm // ck):
        vc = vb[c * ck:(c + 1) * ck, :]
        mixed = sb_ref[...]
        for g in range(cfg.g_a):
            mixed = mixed + jnp.where(lane_group == g, _dot(wms[g], vc), 0.0)
        mixed_chunks.append(mixed)
    mixed = mixed_chunks[0] if len(mixed_chunks) == 1 else jnp.concatenate(mixed_chunks, axis=0)
    o_a = (u * mixed).astype(BF16)
    ga = jax.nn.sigmoid(z3[:, :cfg.d_model])
    gb = jax.nn.sigmoid(z3[:, cfg.d_model:])
    ya = ga * _dot(o_a, pa_ref[...])

    cos, sup, sdn = cos_ref[...], sup_ref[...], sdn_ref[...]
    half = hd // 8
    q = zq * lax.rsqrt(_head_mean(zq * zq, bdq_ref, hd) + eps) * qg_ref[...]
    q = _rope(q, cos, sup, sdn, half)
    k = zk * lax.rsqrt(_head_mean(zk * zk, bdk_ref, hd) + eps) * kg_ref[...]
    k = _rope(k, cos, sup, sdn, half)
    qi = _rope(zqi, cos, sup, sdn, half)
    ki = _rope(z2, cos, sup, sdn, half)[:, :cfg.idx_dim]
    wi_scale = cfg.n_idx_heads ** -0.5

    if sample:
        (k_ref, va_ref, ki_ref, v_ref, q_ref, qi_ref, wi_ref, ya_ref, gb_ref) = outs
        v_ref[...] = v
        q_ref[...] = q
        qi_ref[...] = qi
        wi_ref[...] = z2[:, cfg.idx_dim:cfg.idx_dim + cfg.n_idx_heads] * wi_scale
    else:
        (k_ref, va_ref, ki_ref, qt_ref, qit_ref, kh_ref, vt_ref, kib_ref, wi_ref, ya_ref, gb_ref) = outs
        qt_ref[0] = (q * (LOG2_E * hd ** -0.5)).T.astype(BF16)
        qit_ref[0] = qi.T.astype(BF16)
        vt_ref[0] = zva.T.astype(BF16)
        for hh in range(cfg.n_kv):
            kh_ref[0, hh] = k[:, hh * hd:(hh + 1) * hd].astype(BF16)
        kib_ref[0] = ki.astype(BF16)
        wi_ref[0] = z2.T[cfg.idx_dim:cfg.idx_dim + cfg.n_idx_heads, :] * wi_scale
    k_ref[...] = k
    va_ref[...] = zva
    ki_ref[...] = ki
    ya_ref[...] = ya
    gb_ref[...] = gb


def _rope_tables(pos, cfg):
    hd = cfg.head_dim
    rd = hd // 4
    half = rd // 2
    freqs = cfg.rope_theta ** (-jnp.arange(half, dtype=F32) * 2.0 / rd)
    ang = pos.astype(F32)[:, None] * freqs[None, :]
    cos, sin = jnp.cos(ang), jnp.sin(ang)
    n = pos.shape[0]
    one = jnp.ones((n, hd - rd), F32)
    zero = jnp.zeros((n, hd - rd), F32)
    zh = jnp.zeros((n, half), F32)
    cos_h = jnp.concatenate([cos, cos, one], axis=1)
    sup_h = jnp.concatenate([zh, sin, zero], axis=1)
    sdn_h = jnp.concatenate([-sin, zh, zero], axis=1)
    rep = LANES // hd
    return tuple(jnp.tile(t, (1, rep)) for t in (cos_h, sup_h, sdn_h))


def _proj_weights(lw, cfg):
    d_a, hd = cfg.d_a, cfg.head_dim
    nq, nkv, nqi = cfg.n_heads * hd, cfg.n_kv * hd, cfg.n_idx_heads * cfg.idx_dim
    n1 = 2 * d_a + nq + 2 * nkv + nqi
    n2 = cfg.idx_dim + cfg.n_idx_heads
    w_in = lw["w_in"]
    w1 = w_in[:, :n1].astype(BF16)
    w2 = jnp.pad(w_in[:, n1:n1 + n2], ((0, 0), (0, LANES - n2))).astype(BF16)
    w3 = w_in[:, n1 + n2:].astype(BF16)
    gd = d_a // cfg.g_a

    def block_diag(width):
        hidx = jnp.arange(width) // hd
        return (hidx[:, None] == hidx[None, :]).astype(BF16)

    return dict(
        w1=w1, w2=w2, w3=w3,
        sg=lw["sgu_norm_g"].reshape(1, d_a),
        qg=jnp.tile(lw["q_norm_g"], cfg.n_heads).reshape(1, nq),
        kg=jnp.tile(lw["k_norm_g"], cfg.n_kv).reshape(1, nkv),
        bdq=block_diag(nq), bdk=block_diag(nkv),
        pa=lw["w_branch_a"].astype(BF16),
        wm_p=lw["sgu_w"],
        sb_p=jnp.repeat(lw["sgu_b"].T, gd, axis=1),
    )


def _sample_gate_weights(lw, cfg, n_seq):
    t = cfg.dec_seq
    gd = cfg.d_a // cfg.g_a
    eye = jnp.eye(n_seq, dtype=F32)
    wm = jax.vmap(lambda w: jnp.kron(eye, w))(lw["sgu_w"][:, :t, :t])
    sb = jnp.tile(jnp.repeat(lw["sgu_b"][:, :t].T, gd, axis=1), (n_seq, 1))
    return wm, sb


def _proj(x, mod, g, pw, wm, sb, tabs, cfg, *, tm, rows_per_mod, sample, n_batch=None):
    n, d = x.shape
    per_row = rows_per_mod is None
    row = lambda i: (i, 0)
    if per_row:
        mod_spec = pl.BlockSpec((mod.shape[0], tm, d), lambda i: (0, i, 0))
        tab_spec = pl.BlockSpec((tm, LANES), row)
    else:
        tiles_per_mod = rows_per_mod // tm
        mod_spec = pl.BlockSpec((1, mod.shape[1], d), lambda i: (i // tiles_per_mod, 0, 0))
        tab_spec = pl.BlockSpec((tm, LANES), lambda i: (i % tiles_per_mod, 0))
    consts = [g.reshape(1, d), pw["w1"], pw["w2"], pw["w3"], pw["sg"], wm, sb, pw["qg"], pw["kg"],
              pw["bdq"], pw["bdk"]]
    in_specs = ([pl.BlockSpec((tm, d), row), mod_spec] + [_const_spec(a.shape) for a in consts]
                + [tab_spec] * 3 + [_const_spec(pw["pa"].shape)])
    args = [x, mod] + consts + list(tabs) + [pw["pa"]]
    hd = cfg.head_dim
    nq, nkv, nqi = cfg.n_heads * hd, cfg.n_kv * hd, cfg.n_idx_heads * cfg.idx_dim

    def rows(width, dtype=F32):
        return pl.BlockSpec((tm, width), row), jax.ShapeDtypeStruct((n, width), dtype)

    outs = [rows(nkv), rows(nkv), rows(cfg.idx_dim)]
    if sample:
        outs += [rows(cfg.d_a), rows(nq), rows(nqi), rows(cfg.n_idx_heads)]
    else:
        s = rows_per_mod
        tps = s // tm

        def transposed(width, dtype=BF16):
            return (pl.BlockSpec((1, width, tm), lambda i: (i // tps, 0, i % tps)),
                    jax.ShapeDtypeStruct((n_batch, width, s), dtype))

        outs += [transposed(nq), transposed(nqi),
                 (pl.BlockSpec((1, cfg.n_kv, tm, hd), lambda i: (i // tps, 0, i % tps, 0)),
                  jax.ShapeDtypeStruct((n_batch, cfg.n_kv, s, hd), BF16)),
                 transposed(nkv),
                 (pl.BlockSpec((1, tm, cfg.idx_dim), lambda i: (i // tps, i % tps, 0)),
                  jax.ShapeDtypeStruct((n_batch, s, cfg.idx_dim), BF16)),
                 transposed(cfg.n_idx_heads, F32)]
    outs += [rows(d), rows(d)]
    kern = functools.partial(_proj_kernel, per_row=per_row, sample=sample, cfg=cfg)
    return pl.pallas_call(
        kern,
        grid=(n // tm,),
        in_specs=in_specs,
        out_specs=[o[0] for o in outs],
        out_shape=[o[1] for o in outs],
        compiler_params=_params(1),
        name="proj_sample" if sample else "proj_prompt",
    )(*args)


def _score_key(x):
    b = lax.bitcast_convert_type(x + 0.0, I32)
    return b ^ ((b >> 31) & 0x7FFFFFFF)


SUBLANES = 8
ALL_KEYS = 2 ** 30


class _KeyBlock:
    def __init__(self, ref, key_axis, tk):
        self.ref, self.key_axis, self.tk = ref, key_axis, tk
        self.n_vec = ref.shape[1 - key_axis]
        if key_axis == 1:
            self.unit = (self.n_vec, LANES)
            self.pos = lax.broadcasted_iota(I32, self.unit, 1)
        else:
            self.unit = (SUBLANES, self.n_vec)
            self.pos = lax.broadcasted_iota(I32, (tk, self.n_vec), 0).reshape(
                tk // SUBLANES, SUBLANES, self.n_vec)

    def bcast(self, a):
        return jnp.broadcast_to(a, self.unit)

    def tile(self, off):
        if self.key_axis == 1:
            x = self.ref[:, pl.ds(off, self.tk)]
            return [(x[:, c * LANES:(c + 1) * LANES], self.pos + (off + c * LANES))
                    for c in range(self.tk // LANES)]
        x = self.ref[pl.ds(off, self.tk), :]
        return [(x.reshape(self.tk // SUBLANES, SUBLANES, self.n_vec), self.pos + off)]

    def count(self, n_tiles, pred):
        def body(j, acc):
            for x, pos in self.tile(pl.multiple_of(j * self.tk, self.tk)):
                hits = jnp.where(pred(x, pos), 1, 0)
                acc = acc + (hits if self.key_axis == 1 else jnp.sum(hits, axis=0))
            return acc

        acc = lax.fori_loop(0, n_tiles, body, jnp.zeros(self.unit, I32))
        return jnp.sum(acc.astype(F32), axis=self.key_axis, keepdims=True).astype(I32)


def _select_threshold(kb, n_tiles, k_eff, idx_bits, live=None):
    def count_ge(cand):
        cb = kb.bcast(cand)
        return kb.count(n_tiles, lambda x, _: x >= cb)

    c0 = count_ge(jnp.zeros_like(k_eff))
    t0 = jnp.where(c0 >= k_eff, 0, INT_MIN).astype(I32)
    open0 = jnp.where(c0 == k_eff, 0, 1).astype(I32)
    if live is not None:
        open0 = jnp.where(live, open0, 0)

    def any_open(opn):
        return jnp.max(opn.astype(F32)) > 0.5

    def cond(state):
        it, _, opn = state
        return jnp.logical_and(it < 31, any_open(opn))

    def body(state):
        it, t, opn = state
        cand = t + lax.shift_left(jnp.int32(1), 30 - it)
        c = count_ge(cand)
        t = jnp.where(opn > 0, jnp.where(c >= k_eff, cand, t), t)
        opn = jnp.where(c == k_eff, 0, opn)
        return it + 1, t, opn

    _, t, opn = lax.while_loop(cond, body, (jnp.int32(0), t0, open0))

    def tie_position():
        tb = kb.bcast(t)
        need = k_eff - kb.count(n_tiles, lambda x, _: x > tb)

        def index_step(it, lo):
            cand = lo + lax.shift_left(jnp.int32(1), idx_bits - 1 - it)
            cb = kb.bcast(cand)
            f = kb.count(n_tiles, lambda x, pos: jnp.where(x == tb, pos, ALL_KEYS) < cb)
            return jnp.where(f < need, cand, lo)

        return lax.fori_loop(0, idx_bits, index_step, jnp.zeros_like(k_eff))

    j = lax.cond(any_open(opn), tie_position, lambda: jnp.full(k_eff.shape, ALL_KEYS, I32))
    return t, j


def _selection_bias(x, pos, tb, jb):
    return jnp.where(x > tb, 0.0, jnp.where(x < tb, NEG_BIG, jnp.where(pos <= jb, 0.0, NEG_BIG)))


def _softmax_step(s, pv, m_ref, l_ref, acc_ref, idx, key_axis, exp=jnp.exp):
    m_old = m_ref[idx]
    m_new = jnp.maximum(m_old, jnp.max(s, axis=key_axis, keepdims=True))
    p = exp(s - m_new)
    alpha = exp(m_old - m_new)
    l_ref[idx] = alpha * l_ref[idx] + jnp.sum(p, axis=key_axis, keepdims=True)
    acc_ref[idx] = alpha * acc_ref[idx] + pv(p.astype(BF16))
    m_ref[idx] = m_new


def _prompt_attn_kernel(qit_ref, wt_ref, ki_ref, qt_ref, k_ref, vt_ref, o_ref,
                        sc_ref, m_ref, l_ref, acc_ref, *, qb, topk, idx_bits, cfg):
    i = pl.program_id(1)
    tk = qb
    n_tiles = i + 1
    q_pos = i * qb + lax.broadcasted_iota(I32, (1, qb), 1)
    key_pos = lax.broadcasted_iota(I32, (tk, qb), 0)
    wt = wt_ref[0]
    rep = cfg.n_heads // cfg.n_kv
    hd, di = cfg.head_dim, cfg.idx_dim

    def score_tile(j, carry):
        off = pl.multiple_of(j * tk, tk)
        kib = ki_ref[0, pl.ds(off, tk), :]
        acc = jnp.zeros((tk, qb), F32)
        for h in range(cfg.n_idx_heads):
            acc = acc + wt[h:h + 1, :] * jnp.maximum(_dot(kib, qit_ref[0, h * di:(h + 1) * di, :]), 0.0)
        sc_ref[pl.ds(off, tk), :] = jnp.where(key_pos + off <= q_pos, _score_key(acc), INT_MIN)
        return carry

    lax.fori_loop(0, n_tiles, score_tile, 0)

    kb = _KeyBlock(sc_ref, 0, tk)
    t, jpos = _select_threshold(kb, n_tiles, jnp.minimum(q_pos + 1, topk), idx_bits)

    m_ref[...] = jnp.full(m_ref.shape, NEG_BIG, F32)
    l_ref[...] = jnp.zeros(l_ref.shape, F32)
    acc_ref[...] = jnp.zeros(acc_ref.shape, F32)

    def attend_tile(j, carry):
        off = pl.multiple_of(j * tk, tk)
        bias = _selection_bias(sc_ref[pl.ds(off, tk), :], key_pos + off, t, jpos)
        bias = jnp.concatenate([bias] * rep, axis=1)
        for g in range(cfg.n_kv):
            qt = jnp.concatenate([qt_ref[0, h * hd:(h + 1) * hd, :] for h in range(g * rep, (g + 1) * rep)],
                                 axis=1)
            s = _dot(k_ref[0, g, pl.ds(off, tk), :], qt) + bias
            vt = vt_ref[0, g * hd:(g + 1) * hd, pl.ds(off, tk)]
            _softmax_step(s, lambda p: _dot(vt, p), m_ref, l_ref, acc_ref, g, 0, jnp.exp2)
        return carry

    lax.fori_loop(0, n_tiles, attend_tile, 0)

    heads = []
    for h in range(cfg.n_heads):
        g, r = divmod(h, rep)
        heads.append(acc_ref[g, :, r * qb:(r + 1) * qb] / l_ref[g, :, r * qb:(r + 1) * qb])
    o_ref[0] = jnp.concatenate(heads, axis=0).T.astype(BF16)


def _prompt_attention(qit, wt, kib, qt, kh, vt, cfg, *, qb=256):
    nb, _, s, hd = kh.shape
    qb = min(qb, s)
    topk = min(cfg.topk_max, s // 4)
    rep = cfg.n_heads // cfg.n_kv
    kern = functools.partial(_prompt_attn_kernel, qb=qb, topk=topk,
                             idx_bits=max(1, int(np.ceil(np.log2(s)))), cfg=cfg)
    q_block = lambda b, i: (b, 0, i)

    def per_batch(shape):
        nd = len(shape)
        return pl.BlockSpec((1,) + shape, lambda b, i: (b,) + (0,) * nd, pipeline_mode=pl.Buffered(1))

    return pl.pallas_call(
        kern,
        grid=(nb, s // qb),
        in_specs=[pl.BlockSpec((1, cfg.n_idx_heads * cfg.idx_dim, qb), q_block),
                  pl.BlockSpec((1, cfg.n_idx_heads, qb), q_block),
                  per_batch((s, cfg.idx_dim)),
                  pl.BlockSpec((1, cfg.n_heads * hd, qb), q_block),
                  per_batch((cfg.n_kv, s, hd)),
                  per_batch((cfg.n_kv * hd, s))],
        out_specs=pl.BlockSpec((1, qb, cfg.n_heads * hd), lambda b, i: (b, i, 0)),
        out_shape=jax.ShapeDtypeStruct((nb, s, cfg.n_heads * hd), BF16),
        scratch_shapes=[pltpu.VMEM((s, qb), I32),
                        pltpu.VMEM((cfg.n_kv, 1, rep * qb), F32),
                        pltpu.VMEM((cfg.n_kv, 1, rep * qb), F32),
                        pltpu.VMEM((cfg.n_kv, hd, rep * qb), F32)],
        compiler_params=_params(2),
        name="prompt_attention",
    )(qit, wt, kib, qt, kh, vt)


TOK_PAD = 8
PAGES_PER_STEP = 16


def _samp_index_kernel(pt_ref, qi_ref, w_ref, kit_ref, *refs, n_pages_step, tk, topk, idx_bits, cfg):
    page_refs = refs[:n_pages_step]
    bias_ref, sc_ref = refs[n_pages_step:]
    c = pl.program_id(1)
    n_chunks = pl.num_programs(1)
    nh = cfg.n_idx_heads
    qi = qi_ref[0]
    w = w_ref[0]
    width = n_pages_step * cfg.page_size
    total = sc_ref.shape[1]

    def token_scores(keys_t):
        y = w * jnp.maximum(_dot(qi, keys_t), 0.0)
        return jnp.sum(y.reshape(nh, TOK_PAD, y.shape[1]), axis=0)

    kc = jnp.concatenate([r[...] for r in page_refs], axis=1).astype(BF16)
    sc_ref[:, pl.ds(pl.multiple_of(c * width, width), width)] = _score_key(token_scores(kc))

    @pl.when(c == n_chunks - 1)
    def _():
        tail = kit_ref.shape[2]
        tok = lax.broadcasted_iota(I32, (TOK_PAD, tail), 0)
        new = lax.broadcasted_iota(I32, (TOK_PAD, tail), 1)
        valid = new <= jnp.minimum(tok, cfg.dec_seq - 1)
        sc_ref[:, total - tail:] = jnp.where(valid, _score_key(token_scores(kit_ref[0])), INT_MIN)
        tok1 = lax.broadcasted_iota(I32, (TOK_PAD, 1), 0)
        k_eff = jnp.minimum(cfg.past_len + jnp.minimum(tok1, cfg.dec_seq - 1) + 1, topk)
        kb = _KeyBlock(sc_ref, 1, tk)
        t, jpos = _select_threshold(kb, total // tk, k_eff, idx_bits, live=tok1 < cfg.dec_seq)
        tb, jb = kb.bcast(t), kb.bcast(jpos)
        for jt in range(total // tk):
            for c128, (x, pos) in enumerate(kb.tile(jt * tk)):
                lo = jt * tk + c128 * LANES
                bias_ref[0, :, lo:lo + LANES] = _selection_bias(x, pos, tb, jb)


def _samp_attn_kernel(pt_ref, q_ref, bias_ref, biast_ref, kt_ref, vt_ref, *refs, n_pages_step, cfg):
    k_pages = refs[:n_pages_step]
    v_pages = refs[n_pages_step:2 * n_pages_step]
    o_ref, m_ref, l_ref, acc_ref = refs[2 * n_pages_step:]
    c = pl.program_id(1)
    n_chunks = pl.num_programs(1)
    q = q_ref[0]
    hd = cfg.head_dim
    rep = cfg.n_heads // cfg.n_kv

    @pl.when(c == 0)
    def _():
        m_ref[...] = jnp.full(m_ref.shape, NEG_BIG, F32)
        l_ref[...] = jnp.zeros(l_ref.shape, F32)
        acc_ref[...] = jnp.zeros(acc_ref.shape, F32)

    def step(kt, vt, bias):
        s = _dot(q, kt) + jnp.concatenate([bias] * cfg.n_heads, axis=0)
        _softmax_step(s, lambda p: _dot_nt(p, vt), m_ref, l_ref, acc_ref, 0, 1)

    kc = jnp.concatenate([r[...] for r in k_pages], axis=1).astype(BF16)
    vc = jnp.concatenate([r[...] for r in v_pages], axis=1).astype(BF16)
    step(kc, vc, bias_ref[0])

    @pl.when(c == n_chunks - 1)
    def _():
        step(kt_ref[0], vt_ref[0], biast_ref[0])
        o = acc_ref[0] / l_ref[0]
        for h in range(cfg.n_heads):
            g = h // rep
            o_ref[0, h] = o[h * TOK_PAD:(h + 1) * TOK_PAD, g * hd:(g + 1) * hd]


def _sample_attention(l, q, qi, wi, k_new, v_new, ki_new, cache_k, cache_v, cache_idx_k, page_table, cfg):
    db, n_pages = page_table.shape
    t, hd, ps = cfg.dec_seq, cfg.head_dim, cfg.page_size
    nh, nkv, nih, di = cfg.n_heads, cfg.n_kv, cfg.n_idx_heads, cfg.idx_dim
    rep = nh // nkv
    pstep = min(PAGES_PER_STEP, n_pages)
    n_chunks = n_pages // pstep
    width = pstep * ps
    tail = LANES
    total = cfg.past_len + tail
    topk = min(cfg.topk_max, (cfg.past_len + t) // 4)
    idx_bits = int(np.ceil(np.log2(total)))
    n_lane_groups = total // LANES
    tk = LANES * max(d for d in range(1, 49) if n_lane_groups % d == 0)

    def head_major(a, nheads, dim):
        a = a.reshape(db, t, nheads, dim).transpose(0, 2, 1, 3)
        return jnp.pad(a, ((0, 0), (0, 0), (0, TOK_PAD - t), (0, 0)))

    qi_rows = head_major(qi, nih, di).reshape(db, nih * TOK_PAD, di).astype(BF16)
    w_rows = head_major(wi, nih, 1).reshape(db, nih * TOK_PAD, 1)
    group = (jnp.arange(nh) // rep)[:, None] == jnp.arange(nkv)[None, :]
    q_rows = head_major(q * (hd ** -0.5), nh, hd)[:, :, :, None, :] * group[None, :, None, :, None].astype(F32)
    q_rows = q_rows.reshape(db, nh * TOK_PAD, nkv * hd).astype(BF16)

    def tail_pad(a):
        a = a.reshape(db, t, a.shape[-1]).transpose(0, 2, 1)
        return jnp.pad(a, ((0, 0), (0, 0), (0, tail - t))).astype(BF16)

    ki_tail, k_tail, v_tail = tail_pad(ki_new), tail_pad(k_new), tail_pad(v_new)
    ck = cache_k.transpose(0, 1, 3, 4, 2).reshape(cache_k.shape[:2] + (nkv * hd, ps))
    cv = cache_v.transpose(0, 1, 3, 4, 2).reshape(cache_v.shape[:2] + (nkv * hd, ps))
    cik = cache_idx_k.transpose(0, 1, 3, 2)

    def page_spec(j, w):
        return pl.BlockSpec((None, None, w, ps), lambda b, c, pt: (l, pt[b, c * pstep + j], 0, 0))

    per_seq3 = lambda b, c, pt: (b, 0, 0)
    bias = pl.pallas_call(
        functools.partial(_samp_index_kernel, n_pages_step=pstep, tk=tk, topk=topk, idx_bits=idx_bits, cfg=cfg),
        grid_spec=pltpu.PrefetchScalarGridSpec(
            num_scalar_prefetch=1,
            grid=(db, n_chunks),
            in_specs=[pl.BlockSpec((1, nih * TOK_PAD, di), per_seq3),
                      pl.BlockSpec((1, nih * TOK_PAD, 1), per_seq3),
                      pl.BlockSpec((1, di, tail), per_seq3)]
                     + [page_spec(j, di) for j in range(pstep)],
            out_specs=pl.BlockSpec((1, TOK_PAD, total), per_seq3),
            scratch_shapes=[pltpu.VMEM((TOK_PAD, total), I32)]),
        out_shape=jax.ShapeDtypeStruct((db, TOK_PAD, total), F32),
        compiler_params=_params(2),
        name="sample_index",
    )(page_table, qi_rows, w_rows, ki_tail, *([cik] * pstep))

    out = pl.pallas_call(
        functools.partial(_samp_attn_kernel, n_pages_step=pstep, cfg=cfg),
        grid_spec=pltpu.PrefetchScalarGridSpec(
            num_scalar_prefetch=1,
            grid=(db, n_chunks),
            in_specs=[pl.BlockSpec((1, nh * TOK_PAD, nkv * hd), per_seq3),
                      pl.BlockSpec((1, TOK_PAD, width), lambda b, c, pt: (b, 0, c)),
                      pl.BlockSpec((1, TOK_PAD, tail), lambda b, c, pt: (b, 0, total // tail - 1)),
                      pl.BlockSpec((1, nkv * hd, tail), per_seq3),
                      pl.BlockSpec((1, nkv * hd, tail), per_seq3)]
                     + [page_spec(j, nkv * hd) for j in range(pstep)] * 2,
            out_specs=pl.BlockSpec((1, nh, TOK_PAD, hd), lambda b, c, pt: (b, 0, 0, 0)),
            scratch_shapes=[pltpu.VMEM((1, nh * TOK_PAD, 1), F32),
                            pltpu.VMEM((1, nh * TOK_PAD, 1), F32),
                            pltpu.VMEM((1, nh * TOK_PAD, nkv * hd), F32)]),
        out_shape=jax.ShapeDtypeStruct((db, nh, TOK_PAD, hd), F32),
        compiler_params=_params(2),
        name="sample_attention",
    )(page_table, q_rows, bias, bias, k_tail, v_tail, *([ck] * pstep), *([cv] * pstep))
    return out[:, :, :t, :].transpose(0, 2, 1, 3).reshape(db * t, nh * hd).astype(BF16)


def _forward(cfg, x_prompt, x_sample, c_prompt, c_sample, cache_k, cache_v, cache_idx_k, page_table,
             mod_w, mod_b, norm_g, ffn_w_in, ffn_w_out, w_in, sgu_norm_g, sgu_w, sgu_b,
             q_norm_g, k_norm_g, w_branch_a, w_branch_b, w_out):
    nb, s, d = x_prompt.shape
    db, t, _ = x_sample.shape
    depth = mod_w.shape[0]
    hd = cfg.head_dim
    n_mod = cfg.n_sub * 3

    n_seq = nb + db
    r_pad = -(-n_seq // 8) * 8
    c_all = jnp.pad(jnp.concatenate([c_prompt, c_sample], axis=0), ((0, r_pad - n_seq), (0, 0)))
    mod = _modulation(c_all, mod_w, mod_b).reshape(depth, r_pad, n_mod, d)
    mod_p = mod[:, :nb]
    mod_s = jnp.repeat(mod[:, nb:n_seq], t, axis=1).transpose(0, 2, 1, 3)

    tabs_p = _rope_tables(jnp.arange(s), cfg)
    tabs_s = _rope_tables(jnp.tile(cfg.past_len + jnp.arange(t), db), cfg)

    xp = x_prompt.reshape(nb * s, d)
    xs = x_sample.reshape(db * t, d)
    tm_ffn = min(512, s)
    tm_proj = min(256, s)
    d_ff = ffn_w_out.shape[2]
    leaves = [[] for _ in range(7)]
    for l in range(depth):
        lw = dict(w_in=w_in[l], sgu_norm_g=sgu_norm_g[l], sgu_w=sgu_w[l], sgu_b=sgu_b[l],
                  q_norm_g=q_norm_g[l], k_norm_g=k_norm_g[l], w_branch_a=w_branch_a[l])
        pw = _proj_weights(lw, cfg)
        wm_s, sb_s = _sample_gate_weights(lw, cfg, db)
        ffn_w = [(ffn_w_in[l, j, :, :d_ff].astype(BF16), ffn_w_in[l, j, :, d_ff:].astype(BF16),
                  ffn_w_out[l, j].astype(BF16)) for j in range(2)]
        pb = w_branch_b[l].astype(BF16)
        wo = w_out[l].astype(BF16)

        xp = _ffn(xp, mod_p[l], norm_g[l, 0], *ffn_w[0], cfg, sub=0, tm=tm_ffn, rows_per_mod=s)
        (k_p, va_p, ki_p, qt, qit, kh, vt, kib, wt, ya_p, gb_p) = _proj(
            xp, mod_p[l], norm_g[l, 1], pw, pw["wm_p"], pw["sb_p"], tabs_p, cfg,
            tm=tm_proj, rows_per_mod=s, sample=False, n_batch=nb)
        ob_p = _prompt_attention(qit, wt, kib, qt, kh, vt, cfg)
        xp = _ffn(xp, mod_p[l], norm_g[l, 2], *ffn_w[1], cfg, sub=2, tm=tm_ffn, rows_per_mod=s,
                  merge_args=(ya_p, gb_p, ob_p.reshape(nb * s, -1), pb, wo))

        xs = _ffn(xs, mod_s[l], norm_g[l, 0], *ffn_w[0], cfg, sub=0, tm=db * t, rows_per_mod=None)
        (k_s, va_s, ki_s, v_s, q_s, qi_s, wi_s, ya_s, gb_s) = _proj(
            xs, mod_s[l], norm_g[l, 1], pw, wm_s, sb_s, tabs_s, cfg,
            tm=db * t, rows_per_mod=None, sample=True)
        ob_s = _sample_attention(l, q_s, qi_s, wi_s, k_s, va_s, ki_s, cache_k, cache_v, cache_idx_k,
                                 page_table, cfg)
        xs = _ffn(xs, mod_s[l], norm_g[l, 2], *ffn_w[1], cfg, sub=2, tm=db * t, rows_per_mod=None,
                  merge_args=(ya_s, gb_s, ob_s, pb, wo))

        leaves[0].append(k_p.reshape(nb, s, cfg.n_kv, hd))
        leaves[1].append(va_p.reshape(nb, s, cfg.n_kv, hd))
        leaves[2].append(ki_p.reshape(nb, s, cfg.idx_dim))
        leaves[3].append(k_s.reshape(db, t, cfg.n_kv, hd))
        leaves[4].append(va_s.reshape(db, t, cfg.n_kv, hd))
        leaves[5].append(ki_s.reshape(db, t, cfg.idx_dim))
        leaves[6].append(v_s.reshape(db, t, cfg.d_a))
    return (xp.reshape(nb, s, d), xs.reshape(db, t, d)) + tuple(jnp.stack(v) for v in leaves)


def kernel(x_prompt, x_sample, c_prompt, c_sample, cache_k, cache_v, cache_idx_k, page_table, mod_w, mod_b, norm_g, ffn_w_in, ffn_w_out, w_in, sgu_norm_g, sgu_w, sgu_b, q_norm_g, k_norm_g, w_branch_a, w_branch_b, w_out):
    return _forward(CFG, x_prompt, x_sample, c_prompt, c_sample, cache_k, cache_v, cache_idx_k, page_table,
                    mod_w, mod_b, norm_g, ffn_w_in, ffn_w_out, w_in, sgu_norm_g, sgu_w, sgu_b,
                    q_norm_g, k_norm_g, w_branch_a, w_branch_b, w_out)
```

```python
import collections
import functools

import jax
import jax.numpy as jnp
import numpy as np
from jax import lax
from jax.experimental import pallas as pl
from jax.experimental.pallas import tpu as pltpu

F32 = jnp.float32
BF16 = jnp.bfloat16
I32 = jnp.int32

LANES = 128
INT_MIN = -(2 ** 31)
NEG_BIG = -1e30
LOG2_E = 1.4426950408889634
VMEM_LIMIT = 56 * 1024 * 1024

Config = collections.namedtuple(
    "Config",
    "d_model seq depth dec_seq past_len page_size chunk d_a g_a n_heads head_dim n_kv "
    "n_idx_heads idx_dim topk_max rope_theta d_ff n_sub eps")

CFG = Config(d_model=1024, seq=8192, depth=2, dec_seq=4, past_len=16384, page_size=128, chunk=128,
             d_a=512, g_a=8, n_heads=8, head_dim=64, n_kv=4, n_idx_heads=8, idx_dim=64,
             topk_max=256, rope_theta=500000.0, d_ff=2816, n_sub=3, eps=1e-6)


def _dot(a, b):
    return lax.dot_general(a, b, (((1,), (0,)), ((), ())), preferred_element_type=F32)


def _dot_nt(a, b):
    return lax.dot_general(a, b, (((1,), (1,)), ((), ())), preferred_element_type=F32)


def _const_spec(shape):
    nd = len(shape)
    return pl.BlockSpec(shape, lambda *_: (0,) * nd, pipeline_mode=pl.Buffered(1))


def _params(n_grid):
    return pltpu.CompilerParams(dimension_semantics=("arbitrary",) * n_grid,
                                vmem_limit_bytes=VMEM_LIMIT)


def _mod_kernel(c_ref, w_ref, b_ref, o_ref):
    c = c_ref[...]
    s = (c * jax.nn.sigmoid(c)).astype(BF16)
    o_ref[0] = _dot(s, w_ref[0]) + b_ref[0]


def _modulation(c_all, mod_w, mod_b, tn=1152):
    depth, d, n = mod_w.shape
    r = c_all.shape[0]
    return pl.pallas_call(
        _mod_kernel,
        grid=(depth, n // tn),
        in_specs=[pl.BlockSpec((r, d), lambda l, j: (0, 0)),
                  pl.BlockSpec((1, d, tn), lambda l, j: (l, 0, j)),
                  pl.BlockSpec((1, 1, tn), lambda l, j: (l, 0, j))],
        out_specs=pl.BlockSpec((1, r, tn), lambda l, j: (l, 0, j)),
        out_shape=jax.ShapeDtypeStruct((depth, r, n), F32),
        compiler_params=_params(2),
        name="modulation",
    )(c_all, mod_w.astype(BF16), mod_b.reshape(depth, 1, n))


def _mod_row(mod_ref, idx, per_row):
    if per_row:
        return mod_ref[idx]
    return mod_ref[0, idx:idx + 1, :]


def _adaln(x, g, shift, scale, eps):
    y = x * lax.rsqrt(jnp.mean(x * x, axis=-1, keepdims=True) + eps)
    return (y * g) * (1.0 + scale) + shift


def _swiglu(h, wa_ref, wb_ref, w2_ref, n_chunks):
    d_ff = wa_ref.shape[1]
    cw = d_ff // n_chunks
    acc = None
    for c in range(n_chunks):
        a = _dot(h, wa_ref[:, c * cw:(c + 1) * cw])
        b = _dot(h, wb_ref[:, c * cw:(c + 1) * cw])
        t = ((a * jax.nn.sigmoid(a)) * b).astype(BF16)
        part = _dot(t, w2_ref[c * cw:(c + 1) * cw, :])
        acc = part if acc is None else acc + part
    return acc


def _ffn_kernel(*refs, merge, per_row, sub, n_chunks, eps):
    if merge:
        (x_ref, mod_ref, g_ref, ya_ref, gb_ref, ob_ref, pb_ref, wo_ref,
         wa_ref, wb_ref, w2_ref, o_ref) = refs
    else:
        x_ref, mod_ref, g_ref, wa_ref, wb_ref, w2_ref, o_ref = refs
    x = x_ref[...]
    if merge:
        y = ya_ref[...] + gb_ref[...] * _dot(ob_ref[...], pb_ref[...])
        x = x + _mod_row(mod_ref, 5, per_row) * _dot(y.astype(BF16), wo_ref[...])
    shift = _mod_row(mod_ref, 3 * sub, per_row)
    scale = _mod_row(mod_ref, 3 * sub + 1, per_row)
    gate = _mod_row(mod_ref, 3 * sub + 2, per_row)
    h = _adaln(x, g_ref[...], shift, scale, eps).astype(BF16)
    o_ref[...] = x + (0.5 * gate) * _swiglu(h, wa_ref, wb_ref, w2_ref, n_chunks)


def _ffn(x, mod, g, wa, wb, w2, cfg, *, sub, tm, rows_per_mod, merge_args=None):
    n, d = x.shape
    per_row = rows_per_mod is None
    merge = merge_args is not None
    row = lambda i: (i, 0)
    if per_row:
        mod_spec = pl.BlockSpec((mod.shape[0], tm, d), lambda i: (0, i, 0))
    else:
        tiles_per_mod = rows_per_mod // tm
        mod_spec = pl.BlockSpec((1, mod.shape[1], d), lambda i: (i // tiles_per_mod, 0, 0))
    in_specs = [pl.BlockSpec((tm, d), row), mod_spec, _const_spec((1, d))]
    args = [x, mod, g.reshape(1, d)]
    if merge:
        ya, gb, ob, pb, wo = merge_args
        in_specs += [pl.BlockSpec((tm, d), row), pl.BlockSpec((tm, d), row),
                     pl.BlockSpec((tm, ob.shape[1]), row), _const_spec(pb.shape), _const_spec(wo.shape)]
        args += [ya, gb, ob, pb, wo]
    in_specs += [_const_spec(wa.shape), _const_spec(wb.shape), _const_spec(w2.shape)]
    args += [wa, wb, w2]
    kern = functools.partial(_ffn_kernel, merge=merge, per_row=per_row, sub=sub, n_chunks=2, eps=cfg.eps)
    return pl.pallas_call(
        kern,
        grid=(n // tm,),
        in_specs=in_specs,
        out_specs=pl.BlockSpec((tm, d), row),
        out_shape=jax.ShapeDtypeStruct((n, d), F32),
        compiler_params=_params(1),
        name="merge_ffn" if merge else "ffn",
    )(*args)


def _gelu_tanh(x):
    return 0.5 * x * (1.0 + jnp.tanh(np.sqrt(2.0 / np.pi).astype(np.float32) * (x + 0.044715 * (x * x * x))))


def _head_mean(sq, bd_ref, head_dim):
    hi = sq.astype(BF16)
    lo = (sq - hi.astype(F32)).astype(BF16)
    return (_dot(hi, bd_ref[...]) + _dot(lo, bd_ref[...])) * (1.0 / head_dim)


def _rope(x, cos, sin_up, sin_dn, shift):
    parts = []
    for j in range(x.shape[1] // LANES):
        xs = x[:, j * LANES:(j + 1) * LANES]
        parts.append(xs * cos + pltpu.roll(xs, shift, 1) * sin_up + pltpu.roll(xs, LANES - shift, 1) * sin_dn)
    return parts[0] if len(parts) == 1 else jnp.concatenate(parts, axis=1)


def _proj_kernel(*refs, per_row, sample, cfg):
    (x_ref, mod_ref, g_ref, w1_ref, w2_ref, w3_ref, sg_ref, wm_ref, sb_ref, qg_ref, kg_ref,
     bdq_ref, bdk_ref, cos_ref, sup_ref, sdn_ref, pa_ref) = refs[:17]
    outs = refs[17:]
    eps, hd, d_a = cfg.eps, cfg.head_dim, cfg.d_a
    nq, nkv, nqi = cfg.n_heads * hd, cfg.n_kv * hd, cfg.n_idx_heads * cfg.idx_dim
    tm = x_ref.shape[0]

    x = x_ref[...]
    h = _adaln(x, g_ref[...], _mod_row(mod_ref, 3, per_row), _mod_row(mod_ref, 4, per_row), eps).astype(BF16)
    z1 = _dot(h, w1_ref[...])
    z2 = _dot(h, w2_ref[...])
    z3 = _dot(h, w3_ref[...])
    o = 0
    zu = z1[:, o:o + d_a]; o += d_a
    zv = z1[:, o:o + d_a]; o += d_a
    zq = z1[:, o:o + nq]; o += nq
    zk = z1[:, o:o + nkv]; o += nkv
    zva = z1[:, o:o + nkv]; o += nkv
    zqi = z1[:, o:o + nqi]

    u = _gelu_tanh(zu)
    gv = _gelu_tanh(zv)
    v = gv * lax.rsqrt(jnp.mean(gv * gv, axis=-1, keepdims=True) + eps) * sg_ref[...]
    vb = v.astype(BF16)
    ck = wm_ref.shape[1]
    rr = lax.broadcasted_iota(I32, (ck, ck), 0)
    cc = lax.broadcasted_iota(I32, (ck, ck), 1)
    lane_group = lax.broadcasted_iota(I32, (ck, d_a), 1) // (d_a // cfg.g_a)
    wms = [jnp.where(rr >= cc, wm_ref[g], 0.0).astype(BF16) for g in range(cfg.g_a)]
    mixed_chunks = []
    for c in range(tm // ck):
        vc = vb[c * ck:(c + 1) * ck, :]
        mixed = sb_ref[...]
        for g in range(cfg.g_a):
            mixed = mixed + jnp.where(lane_group == g, _dot(wms[g], vc), 0.0)
        mixed_chunks.append(mixed)
    mixed = mixed_chunks[0] if len(mixed_chunks) == 1 else jnp.concatenate(mixed_chunks, axis=0)
    o_a = (u * mixed).astype(BF16)
    ga = jax.nn.sigmoid(z3[:, :cfg.d_model])
    gb = jax.nn.sigmoid(z3[:, cfg.d_model:])
    ya = ga * _dot(o_a, pa_ref[...])

    cos, sup, sdn = cos_ref[...], sup_ref[...], sdn_ref[...]
    half = hd // 8
    q = zq * lax.rsqrt(_head_mean(zq * zq, bdq_ref, hd) + eps) * qg_ref[...]
    q = _rope(q, cos, sup, sdn, half)
    k = zk * lax.rsqrt(_head_mean(zk * zk, bdk_ref, hd) + eps) * kg_ref[...]
    k = _rope(k, cos, sup, sdn, half)
    qi = _rope(zqi, cos, sup, sdn, half)
    ki = _rope(z2, cos, sup, sdn, half)[:, :cfg.idx_dim]
    wi_scale = cfg.n_idx_heads ** -0.5

    if sample:
        (k_ref, va_ref, ki_ref, v_ref, q_ref, qi_ref, wi_ref, ya_ref, gb_ref) = outs
        v_ref[...] = v
        q_ref[...] = q
        qi_ref[...] = qi
        wi_ref[...] = z2[:, cfg.idx_dim:cfg.idx_dim + cfg.n_idx_heads] * wi_scale
    else:
        (k_ref, va_ref, ki_ref, qt_ref, qit_ref, kh_ref, vt_ref, kib_ref, wi_ref, ya_ref, gb_ref) = outs
        qt_ref[0] = (q * (LOG2_E * hd ** -0.5)).T.astype(BF16)
        qit_ref[0] = qi.T.astype(BF16)
        vt_ref[0] = zva.T.astype(BF16)
        for hh in range(cfg.n_kv):
            kh_ref[0, hh] = k[:, hh * hd:(hh + 1) * hd].astype(BF16)
        kib_ref[0] = ki.astype(BF16)
        wi_ref[0] = z2.T[cfg.idx_dim:cfg.idx_dim + cfg.n_idx_heads, :] * wi_scale
    k_ref[...] = k
    va_ref[...] = zva
    ki_ref[...] = ki
    ya_ref[...] = ya
    gb_ref[...] = gb


def _rope_tables(pos, cfg):
    hd = cfg.head_dim
    rd = hd // 4
    half = rd // 2
    freqs = cfg.rope_theta ** (-jnp.arange(half, dtype=F32) * 2.0 / rd)
    ang = pos.astype(F32)[:, None] * freqs[None, :]
    cos, sin = jnp.cos(ang), jnp.sin(ang)
    n = pos.shape[0]
    one = jnp.ones((n, hd - rd), F32)
    zero = jnp.zeros((n, hd - rd), F32)
    zh = jnp.zeros((n, half), F32)
    cos_h = jnp.concatenate([cos, cos, one], axis=1)
    sup_h = jnp.concatenate([zh, sin, zero], axis=1)
    sdn_h = jnp.concatenate([-sin, zh, zero], axis=1)
    rep = LANES // hd
    return tuple(jnp.tile(t, (1, rep)) for t in (cos_h, sup_h, sdn_h))


def _proj_weights(lw, cfg):
    d_a, hd = cfg.d_a, cfg.head_dim
    nq, nkv, nqi = cfg.n_heads * hd, cfg.n_kv * hd, cfg.n_idx_heads * cfg.idx_dim
    n1 = 2 * d_a + nq + 2 * nkv + nqi
    n2 = cfg.idx_dim + cfg.n_idx_heads
    w_in = lw["w_in"]
    w1 = w_in[:, :n1].astype(BF16)
    w2 = jnp.pad(w_in[:, n1:n1 + n2], ((0, 0), (0, LANES - n2))).astype(BF16)
    w3 = w_in[:, n1 + n2:].astype(BF16)
    gd = d_a // cfg.g_a

    def block_diag(width):
        hidx = jnp.arange(width) // hd
        return (hidx[:, None] == hidx[None, :]).astype(BF16)

    return dict(
        w1=w1, w2=w2, w3=w3,
        sg=lw["sgu_norm_g"].reshape(1, d_a),
        qg=jnp.tile(lw["q_norm_g"], cfg.n_heads).reshape(1, nq),
        kg=jnp.tile(lw["k_norm_g"], cfg.n_kv).reshape(1, nkv),
        bdq=block_diag(nq), bdk=block_diag(nkv),
        pa=lw["w_branch_a"].astype(BF16),
        wm_p=lw["sgu_w"],
        sb_p=jnp.repeat(lw["sgu_b"].T, gd, axis=1),
    )


def _sample_gate_weights(lw, cfg, n_seq):
    t = cfg.dec_seq
    gd = cfg.d_a // cfg.g_a
    eye = jnp.eye(n_seq, dtype=F32)
    wm = jax.vmap(lambda w: jnp.kron(eye, w))(lw["sgu_w"][:, :t, :t])
    sb = jnp.tile(jnp.repeat(lw["sgu_b"][:, :t].T, gd, axis=1), (n_seq, 1))
    return wm, sb


def _proj(x, mod, g, pw, wm, sb, tabs, cfg, *, tm, rows_per_mod, sample, n_batch=None):
    n, d = x.shape
    per_row = rows_per_mod is None
    row = lambda i: (i, 0)
    if per_row:
        mod_spec = pl.BlockSpec((mod.shape[0], tm, d), lambda i: (0, i, 0))
        tab_spec = pl.BlockSpec((tm, LANES), row)
    else:
        tiles_per_mod = rows_per_mod // tm
        mod_spec = pl.BlockSpec((1, mod.shape[1], d), lambda i: (i // tiles_per_mod, 0, 0))
        tab_spec = pl.BlockSpec((tm, LANES), lambda i: (i % tiles_per_mod, 0))
    consts = [g.reshape(1, d), pw["w1"], pw["w2"], pw["w3"], pw["sg"], wm, sb, pw["qg"], pw["kg"],
              pw["bdq"], pw["bdk"]]
    in_specs = ([pl.BlockSpec((tm, d), row), mod_spec] + [_const_spec(a.shape) for a in consts]
                + [tab_spec] * 3 + [_const_spec(pw["pa"].shape)])
    args = [x, mod] + consts + list(tabs) + [pw["pa"]]
    hd = cfg.head_dim
    nq, nkv, nqi = cfg.n_heads * hd, cfg.n_kv * hd, cfg.n_idx_heads * cfg.idx_dim

    def rows(width, dtype=F32):
        return pl.BlockSpec((tm, width), row), jax.ShapeDtypeStruct((n, width), dtype)

    outs = [rows(nkv), rows(nkv), rows(cfg.idx_dim)]
    if sample:
        outs += [rows(cfg.d_a), rows(nq), rows(nqi), rows(cfg.n_idx_heads)]
    else:
        s = rows_per_mod
        tps = s // tm

        def transposed(width, dtype=BF16):
            return (pl.BlockSpec((1, width, tm), lambda i: (i // tps, 0, i % tps)),
                    jax.ShapeDtypeStruct((n_batch, width, s), dtype))

        outs += [transposed(nq), transposed(nqi),
                 (pl.BlockSpec((1, cfg.n_kv, tm, hd), lambda i: (i // tps, 0, i % tps, 0)),
                  jax.ShapeDtypeStruct((n_batch, cfg.n_kv, s, hd), BF16)),
                 transposed(nkv),
                 (pl.BlockSpec((1, tm, cfg.idx_dim), lambda i: (i // tps, i % tps, 0)),
                  jax.ShapeDtypeStruct((n_batch, s, cfg.idx_dim), BF16)),
                 transposed(cfg.n_idx_heads, F32)]
    outs += [rows(d), rows(d)]
    kern = functools.partial(_proj_kernel, per_row=per_row, sample=sample, cfg=cfg)
    return pl.pallas_call(
        kern,
        grid=(n // tm,),
        in_specs=in_specs,
        out_specs=[o[0] for o in outs],
        out_shape=[o[1] for o in outs],
        compiler_params=_params(1),
        name="proj_sample" if sample else "proj_prompt",
    )(*args)


def _score_key(x):
    b = lax.bitcast_convert_type(x + 0.0, I32)
    return b ^ ((b >> 31) & 0x7FFFFFFF)


SUBLANES = 8
BF16_ROWS = 16
ALL_KEYS = 2 ** 30


class _KeyBlock:
    def __init__(self, ref, key_axis, tk, tiles_per_trip=1):
        self.ref, self.key_axis, self.tk, self.tiles_per_trip = ref, key_axis, tk, tiles_per_trip
        self.n_vec = ref.shape[1 - key_axis]
        if key_axis == 1:
            self.unit = (self.n_vec, LANES)
            self.pos = lax.broadcasted_iota(I32, self.unit, 1)
        else:
            self.unit = (SUBLANES, self.n_vec)
            self.pos = lax.broadcasted_iota(I32, (tk, self.n_vec), 0).reshape(
                tk // SUBLANES, SUBLANES, self.n_vec)

    def bcast(self, a):
        return jnp.broadcast_to(a, self.unit)

    def tile(self, off):
        if self.key_axis == 1:
            x = self.ref[:, pl.ds(off, self.tk)]
            return [(x[:, c * LANES:(c + 1) * LANES], self.pos + (off + c * LANES))
                    for c in range(self.tk // LANES)]
        x = self.ref[pl.ds(off, self.tk), :]
        return [(x.reshape(self.tk // SUBLANES, SUBLANES, self.n_vec), self.pos + off)]

    def count(self, n_tiles, pred):
        group = self.tiles_per_trip

        def body(j, acc):
            for u in range(group):
                for x, pos in self.tile(pl.multiple_of((j * group + u) * self.tk, self.tk)):
                    hits = jnp.where(pred(x, pos), 1, 0)
                    acc = acc + (hits if self.key_axis == 1 else jnp.sum(hits, axis=0))
            return acc

        n_trips = n_tiles if group == 1 else (n_tiles + group - 1) // group
        acc = lax.fori_loop(0, n_trips, body, jnp.zeros(self.unit, I32))
        return jnp.sum(acc.astype(F32), axis=self.key_axis, keepdims=True).astype(I32)


def _select_threshold(kb, n_tiles, k_eff, idx_bits, live=None):
    def count_ge(cand):
        cb = kb.bcast(cand)
        return kb.count(n_tiles, lambda x, _: x >= cb)

    c0 = count_ge(jnp.zeros_like(k_eff))
    t0 = jnp.where(c0 >= k_eff, 0, INT_MIN).astype(I32)
    open0 = jnp.where(c0 == k_eff, 0, 1).astype(I32)
    if live is not None:
        open0 = jnp.where(live, open0, 0)

    def any_open(opn):
        return jnp.max(opn.astype(F32)) > 0.5

    def cond(state):
        it, _, opn = state
        return jnp.logical_and(it < 31, any_open(opn))

    def body(state):
        it, t, opn = state
        cand = t + lax.shift_left(jnp.int32(1), 30 - it)
        c = count_ge(cand)
        t = jnp.where(opn > 0, jnp.where(c >= k_eff, cand, t), t)
        opn = jnp.where(c == k_eff, 0, opn)
        return it + 1, t, opn

    _, t, opn = lax.while_loop(cond, body, (jnp.int32(0), t0, open0))

    def tie_position():
        tb = kb.bcast(t)
        need = k_eff - kb.count(n_tiles, lambda x, _: x > tb)

        def index_step(it, lo):
            cand = lo + lax.shift_left(jnp.int32(1), idx_bits - 1 - it)
            cb = kb.bcast(cand)
            f = kb.count(n_tiles, lambda x, pos: jnp.where(x == tb, pos, ALL_KEYS) < cb)
            return jnp.where(f < need, cand, lo)

        return lax.fori_loop(0, idx_bits, index_step, jnp.zeros_like(k_eff))

    j = lax.cond(any_open(opn), tie_position, lambda: jnp.full(k_eff.shape, ALL_KEYS, I32))
    return t, j


def _selection_bias(x, pos, tb, jb):
    return jnp.where(x > tb, 0.0, jnp.where(x < tb, NEG_BIG, jnp.where(pos <= jb, 0.0, NEG_BIG)))


def _softmax_step(s, pv, m_ref, l_ref, acc_ref, idx, key_axis, exp=jnp.exp):
    m_old = m_ref[idx]
    m_new = jnp.maximum(m_old, jnp.max(s, axis=key_axis, keepdims=True))
    p = exp(s - m_new)
    alpha = exp(m_old - m_new)
    l_ref[idx] = alpha * l_ref[idx] + jnp.sum(p, axis=key_axis, keepdims=True)
    acc_ref[idx] = alpha * acc_ref[idx] + pv(p.astype(BF16))
    m_ref[idx] = m_new


def _prompt_attn_kernel(qit_ref, wt_ref, ki_ref, qt_ref, k_ref, vt_ref, o_ref,
                        sc_ref, s_ref, m_ref, acc_ref, *, qb, topk, idx_bits, cfg):
    i = pl.program_id(1)
    tk = qb
    n_tiles = i + 1
    q_pos = i * qb + lax.broadcasted_iota(I32, (1, qb), 1)
    key_pos = lax.broadcasted_iota(I32, (tk, qb), 0)
    wt = wt_ref[0]
    rep = cfg.n_heads // cfg.n_kv
    hd, di = cfg.head_dim, cfg.idx_dim

    def score_tile(j, carry):
        off = pl.multiple_of(j * tk, tk)
        kib = ki_ref[0, pl.ds(off, tk), :]
        acc = jnp.zeros((tk, qb), F32)
        for h in range(cfg.n_idx_heads):
            acc = acc + wt[h:h + 1, :] * jnp.maximum(_dot(kib, qit_ref[0, h * di:(h + 1) * di, :]), 0.0)
        sc_ref[pl.ds(off, tk), :] = jnp.where(key_pos + off <= q_pos, _score_key(acc), INT_MIN)
        return carry

    lax.fori_loop(0, n_tiles, score_tile, 0)

    pair = 2 if (sc_ref.shape[0] // tk) % 2 == 0 else 1
    if pair == 2:
        @pl.when(n_tiles % 2 == 1)
        def _():
            sc_ref[pl.ds(pl.multiple_of(n_tiles * tk, tk), tk), :] = jnp.full((tk, qb), INT_MIN, I32)

    kb = _KeyBlock(sc_ref, 0, tk, tiles_per_trip=pair)
    t, jpos = _select_threshold(kb, n_tiles, jnp.minimum(q_pos + 1, topk), idx_bits)

    m_ref[...] = jnp.full(m_ref.shape, NEG_BIG, F32)
    acc_ref[...] = jnp.zeros(acc_ref.shape, F32)
    ones_rows = jnp.where(lax.broadcasted_iota(I32, (BF16_ROWS, tk), 0) == 0, 1.0, 0.0).astype(BF16)

    def attend_tile(j, carry):
        off = pl.multiple_of(j * tk, tk)
        bias = _selection_bias(sc_ref[pl.ds(off, tk), :], key_pos + off, t, jpos)
        bias = jnp.concatenate([bias] * rep, axis=1)
        for g in range(cfg.n_kv):
            qt = jnp.concatenate([qt_ref[0, h * hd:(h + 1) * hd, :] for h in range(g * rep, (g + 1) * rep)],
                                 axis=1)
            s_ref[g] = _dot(k_ref[0, g, pl.ds(off, tk), :], qt) + bias
        for g in range(cfg.n_kv):
            vt = jnp.concatenate([vt_ref[0, g * hd:(g + 1) * hd, pl.ds(off, tk)], ones_rows], axis=0)
            s = s_ref[g]
            m_old = m_ref[g]
            m_new = jnp.maximum(m_old, jnp.max(s, axis=0, keepdims=True))
            p = jnp.exp2(s - m_new).astype(BF16)
            acc_ref[g] = jnp.exp2(m_old - m_new) * acc_ref[g] + _dot(vt, p)
            m_ref[g] = m_new
        return carry

    lax.fori_loop(0, n_tiles, attend_tile, 0)

    heads = []
    for h in range(cfg.n_heads):
        g, r = divmod(h, rep)
        a = acc_ref[g, :, r * qb:(r + 1) * qb]
        heads.append(a[:hd] / a[hd:hd + 1])
    o_ref[0] = jnp.concatenate(heads, axis=0).T.astype(BF16)


def _prompt_attention(qit, wt, kib, qt, kh, vt, cfg, *, qb=256):
    nb, _, s, hd = kh.shape
    qb = min(qb, s)
    topk = min(cfg.topk_max, s // 4)
    rep = cfg.n_heads // cfg.n_kv
    kern = functools.partial(_prompt_attn_kernel, qb=qb, topk=topk,
                             idx_bits=max(1, int(np.ceil(np.log2(s)))), cfg=cfg)
    q_block = lambda b, i: (b, 0, i)

    def per_batch(shape):
        nd = len(shape)
        return pl.BlockSpec((1,) + shape, lambda b, i: (b,) + (0,) * nd, pipeline_mode=pl.Buffered(1))

    return pl.pallas_call(
        kern,
        grid=(nb, s // qb),
        in_specs=[pl.BlockSpec((1, cfg.n_idx_heads * cfg.idx_dim, qb), q_block),
                  pl.BlockSpec((1, cfg.n_idx_heads, qb), q_block),
                  per_batch((s, cfg.idx_dim)),
                  pl.BlockSpec((1, cfg.n_heads * hd, qb), q_block),
                  per_batch((cfg.n_kv, s, hd)),
                  per_batch((cfg.n_kv * hd, s))],
        out_specs=pl.BlockSpec((1, qb, cfg.n_heads * hd), lambda b, i: (b, i, 0)),
        out_shape=jax.ShapeDtypeStruct((nb, s, cfg.n_heads * hd), BF16),
        scratch_shapes=[pltpu.VMEM((s, qb), I32),
                        pltpu.VMEM((cfg.n_kv, qb, rep * qb), F32),
                        pltpu.VMEM((cfg.n_kv, 1, rep * qb), F32),
                        pltpu.VMEM((cfg.n_kv, hd + BF16_ROWS, rep * qb), F32)],
        compiler_params=_params(2),
        name="prompt_attention",
    )(qit, wt, kib, qt, kh, vt)


TOK_PAD = 8
PAGES_PER_STEP = 16


def _samp_index_kernel(pt_ref, qi_ref, w_ref, kit_ref, *refs, n_pages_step, tk, topk, idx_bits, cfg):
    page_refs = refs[:n_pages_step]
    bias_ref, sc_ref = refs[n_pages_step:]
    c = pl.program_id(1)
    n_chunks = pl.num_programs(1)
    nh = cfg.n_idx_heads
    qi = qi_ref[0]
    w = w_ref[0]
    width = n_pages_step * cfg.page_size
    total = sc_ref.shape[1]

    def token_scores(keys_t):
        y = w * jnp.maximum(_dot(qi, keys_t), 0.0)
        return jnp.sum(y.reshape(nh, TOK_PAD, y.shape[1]), axis=0)

    kc = jnp.concatenate([r[...] for r in page_refs], axis=1).astype(BF16)
    sc_ref[:, pl.ds(pl.multiple_of(c * width, width), width)] = _score_key(token_scores(kc))

    @pl.when(c == n_chunks - 1)
    def _():
        tail = kit_ref.shape[2]
        tok = lax.broadcasted_iota(I32, (TOK_PAD, tail), 0)
        new = lax.broadcasted_iota(I32, (TOK_PAD, tail), 1)
        valid = new <= jnp.minimum(tok, cfg.dec_seq - 1)
        sc_ref[:, total - tail:] = jnp.where(valid, _score_key(token_scores(kit_ref[0])), INT_MIN)
        tok1 = lax.broadcasted_iota(I32, (TOK_PAD, 1), 0)
        k_eff = jnp.minimum(cfg.past_len + jnp.minimum(tok1, cfg.dec_seq - 1) + 1, topk)
        kb = _KeyBlock(sc_ref, 1, tk)
        t, jpos = _select_threshold(kb, total // tk, k_eff, idx_bits, live=tok1 < cfg.dec_seq)
        tb, jb = kb.bcast(t), kb.bcast(jpos)
        for jt in range(total // tk):
            for c128, (x, pos) in enumerate(kb.tile(jt * tk)):
                lo = jt * tk + c128 * LANES
                bias_ref[0, :, lo:lo + LANES] = _selection_bias(x, pos, tb, jb)


def _samp_attn_kernel(pt_ref, q_ref, bias_ref, biast_ref, kt_ref, vt_ref, *refs, n_pages_step, cfg):
    k_pages = refs[:n_pages_step]
    v_pages = refs[n_pages_step:2 * n_pages_step]
    o_ref, m_ref, l_ref, acc_ref = refs[2 * n_pages_step:]
    c = pl.program_id(1)
    n_chunks = pl.num_programs(1)
    q = q_ref[0]
    hd = cfg.head_dim
    rep = cfg.n_heads // cfg.n_kv

    @pl.when(c == 0)
    def _():
        m_ref[...] = jnp.full(m_ref.shape, NEG_BIG, F32)
        l_ref[...] = jnp.zeros(l_ref.shape, F32)
        acc_ref[...] = jnp.zeros(acc_ref.shape, F32)

    def step(kt, vt, bias):
        s = _dot(q, kt) + jnp.concatenate([bias] * cfg.n_heads, axis=0)
        _softmax_step(s, lambda p: _dot_nt(p, vt), m_ref, l_ref, acc_ref, 0, 1)

    kc = jnp.concatenate([r[...] for r in k_pages], axis=1).astype(BF16)
    vc = jnp.concatenate([r[...] for r in v_pages], axis=1).astype(BF16)
    step(kc, vc, bias_ref[0])

    @pl.when(c == n_chunks - 1)
    def _():
        step(kt_ref[0], vt_ref[0], biast_ref[0])
        o = acc_ref[0] / l_ref[0]
        for h in range(cfg.n_heads):
            g = h // rep
            o_ref[0, h] = o[h * TOK_PAD:(h + 1) * TOK_PAD, g * hd:(g + 1) * hd]


def _sample_attention(l, q, qi, wi, k_new, v_new, ki_new, cache_k, cache_v, cache_idx_k, page_table, cfg):
    db, n_pages = page_table.shape
    t, hd, ps = cfg.dec_seq, cfg.head_dim, cfg.page_size
    nh, nkv, nih, di = cfg.n_heads, cfg.n_kv, cfg.n_idx_heads, cfg.idx_dim
    rep = nh // nkv
    pstep = min(PAGES_PER_STEP, n_pages)
    n_chunks = n_pages // pstep
    width = pstep * ps
    tail = LANES
    total = cfg.past_len + tail
    topk = min(cfg.topk_max, (cfg.past_len + t) // 4)
    idx_bits = int(np.ceil(np.log2(total)))
    n_lane_groups = total // LANES
    tk = LANES * max(d for d in range(1, 49) if n_lane_groups % d == 0)

    def head_major(a, nheads, dim):
        a = a.reshape(db, t, nheads, dim).transpose(0, 2, 1, 3)
        return jnp.pad(a, ((0, 0), (0, 0), (0, TOK_PAD - t), (0, 0)))

    qi_rows = head_major(qi, nih, di).reshape(db, nih * TOK_PAD, di).astype(BF16)
    w_rows = head_major(wi, nih, 1).reshape(db, nih * TOK_PAD, 1)
    group = (jnp.arange(nh) // rep)[:, None] == jnp.arange(nkv)[None, :]
    q_rows = head_major(q * (hd ** -0.5), nh, hd)[:, :, :, None, :] * group[None, :, None, :, None].astype(F32)
    q_rows = q_rows.reshape(db, nh * TOK_PAD, nkv * hd).astype(BF16)

    def tail_pad(a):
        a = a.reshape(db, t, a.shape[-1]).transpose(0, 2, 1)
        return jnp.pad(a, ((0, 0), (0, 0), (0, tail - t))).astype(BF16)

    ki_tail, k_tail, v_tail = tail_pad(ki_new), tail_pad(k_new), tail_pad(v_new)
    ck = cache_k.transpose(0, 1, 3, 4, 2).reshape(cache_k.shape[:2] + (nkv * hd, ps))
    cv = cache_v.transpose(0, 1, 3, 4, 2).reshape(cache_v.shape[:2] + (nkv * hd, ps))
    cik = cache_idx_k.transpose(0, 1, 3, 2)

    def page_spec(j, w):
        return pl.BlockSpec((None, None, w, ps), lambda b, c, pt: (l, pt[b, c * pstep + j], 0, 0))

    per_seq3 = lambda b, c, pt: (b, 0, 0)
    bias = pl.pallas_call(
        functools.partial(_samp_index_kernel, n_pages_step=pstep, tk=tk, topk=topk, idx_bits=idx_bits, cfg=cfg),
        grid_spec=pltpu.PrefetchScalarGridSpec(
            num_scalar_prefetch=1,
            grid=(db, n_chunks),
            in_specs=[pl.BlockSpec((1, nih * TOK_PAD, di), per_seq3),
                      pl.BlockSpec((1, nih * TOK_PAD, 1), per_seq3),
                      pl.BlockSpec((1, di, tail), per_seq3)]
                     + [page_spec(j, di) for j in range(pstep)],
            out_specs=pl.BlockSpec((1, TOK_PAD, total), per_seq3),
            scratch_shapes=[pltpu.VMEM((TOK_PAD, total), I32)]),
        out_shape=jax.ShapeDtypeStruct((db, TOK_PAD, total), F32),
        compiler_params=_params(2),
        name="sample_index",
    )(page_table, qi_rows, w_rows, ki_tail, *([cik] * pstep))

    out = pl.pallas_call(
        functools.partial(_samp_attn_kernel, n_pages_step=pstep, cfg=cfg),
        grid_spec=pltpu.PrefetchScalarGridSpec(
            num_scalar_prefetch=1,
            grid=(db, n_chunks),
            in_specs=[pl.BlockSpec((1, nh * TOK_PAD, nkv * hd), per_seq3),
                      pl.BlockSpec((1, TOK_PAD, width), lambda b, c, pt: (b, 0, c)),
                      pl.BlockSpec((1, TOK_PAD, tail), lambda b, c, pt: (b, 0, total // tail - 1)),
                      pl.BlockSpec((1, nkv * hd, tail), per_seq3),
                      pl.BlockSpec((1, nkv * hd, tail), per_seq3)]
                     + [page_spec(j, nkv * hd) for j in range(pstep)] * 2,
            out_specs=pl.BlockSpec((1, nh, TOK_PAD, hd), lambda b, c, pt: (b, 0, 0, 0)),
            scratch_shapes=[pltpu.VMEM((1, nh * TOK_PAD, 1), F32),
                            pltpu.VMEM((1, nh * TOK_PAD, 1), F32),
                            pltpu.VMEM((1, nh * TOK_PAD, nkv * hd), F32)]),
        out_shape=jax.ShapeDtypeStruct((db, nh, TOK_PAD, hd), F32),
        compiler_params=_params(2),
        name="sample_attention",
    )(page_table, q_rows, bias, bias, k_tail, v_tail, *([ck] * pstep), *([cv] * pstep))
    return out[:, :, :t, :].transpose(0, 2, 1, 3).reshape(db * t, nh * hd).astype(BF16)


def _forward(cfg, x_prompt, x_sample, c_prompt, c_sample, cache_k, cache_v, cache_idx_k, page_table,
             mod_w, mod_b, norm_g, ffn_w_in, ffn_w_out, w_in, sgu_norm_g, sgu_w, sgu_b,
             q_norm_g, k_norm_g, w_branch_a, w_branch_b, w_out):
    nb, s, d = x_prompt.shape
    db, t, _ = x_sample.shape
    depth = mod_w.shape[0]
    hd = cfg.head_dim
    n_mod = cfg.n_sub * 3

    n_seq = nb + db
    r_pad = -(-n_seq // 8) * 8
    c_all = jnp.pad(jnp.concatenate([c_prompt, c_sample], axis=0), ((0, r_pad - n_seq), (0, 0)))
    mod = _modulation(c_all, mod_w, mod_b).reshape(depth, r_pad, n_mod, d)
    mod_p = mod[:, :nb]
    mod_s = jnp.repeat(mod[:, nb:n_seq], t, axis=1).transpose(0, 2, 1, 3)

    tabs_p = _rope_tables(jnp.arange(s), cfg)
    tabs_s = _rope_tables(jnp.tile(cfg.past_len + jnp.arange(t), db), cfg)

    xp = x_prompt.reshape(nb * s, d)
    xs = x_sample.reshape(db * t, d)
    tm_ffn = min(512, s)
    tm_proj = min(256, s)
    d_ff = ffn_w_out.shape[2]
    leaves = [[] for _ in range(7)]
    for l in range(depth):
        lw = dict(w_in=w_in[l], sgu_norm_g=sgu_norm_g[l], sgu_w=sgu_w[l], sgu_b=sgu_b[l],
                  q_norm_g=q_norm_g[l], k_norm_g=k_norm_g[l], w_branch_a=w_branch_a[l])
        pw = _proj_weights(lw, cfg)
        wm_s, sb_s = _sample_gate_weights(lw, cfg, db)
        ffn_w = [(ffn_w_in[l, j, :, :d_ff].astype(BF16), ffn_w_in[l, j, :, d_ff:].astype(BF16),
                  ffn_w_out[l, j].astype(BF16)) for j in range(2)]
        pb = w_branch_b[l].astype(BF16)
        wo = w_out[l].astype(BF16)

        xp = _ffn(xp, mod_p[l], norm_g[l, 0], *ffn_w[0], cfg, sub=0, tm=tm_ffn, rows_per_mod=s)
        (k_p, va_p, ki_p, qt, qit, kh, vt, kib, wt, ya_p, gb_p) = _proj(
            xp, mod_p[l], norm_g[l, 1], pw, pw["wm_p"], pw["sb_p"], tabs_p, cfg,
            tm=tm_proj, rows_per_mod=s, sample=False, n_batch=nb)
        ob_p = _prompt_attention(qit, wt, kib, qt, kh, vt, cfg)
        xp = _ffn(xp, mod_p[l], norm_g[l, 2], *ffn_w[1], cfg, sub=2, tm=tm_ffn, rows_per_mod=s,
                  merge_args=(ya_p, gb_p, ob_p.reshape(nb * s, -1), pb, wo))

        xs = _ffn(xs, mod_s[l], norm_g[l, 0], *ffn_w[0], cfg, sub=0, tm=db * t, rows_per_mod=None)
        (k_s, va_s, ki_s, v_s, q_s, qi_s, wi_s, ya_s, gb_s) = _proj(
            xs, mod_s[l], norm_g[l, 1], pw, wm_s, sb_s, tabs_s, cfg,
            tm=db * t, rows_per_mod=None, sample=True)
        ob_s = _sample_attention(l, q_s, qi_s, wi_s, k_s, va_s, ki_s, cache_k, cache_v, cache_idx_k,
                                 page_table, cfg)
        xs = _ffn(xs, mod_s[l], norm_g[l, 2], *ffn_w[1], cfg, sub=2, tm=db * t, rows_per_mod=None,
                  merge_args=(ya_s, gb_s, ob_s, pb, wo))

        leaves[0].append(k_p.reshape(nb, s, cfg.n_kv, hd))
        leaves[1].append(va_p.reshape(nb, s, cfg.n_kv, hd))
        leaves[2].append(ki_p.reshape(nb, s, cfg.idx_dim))
        leaves[3].append(k_s.reshape(db, t, cfg.n_kv, hd))
        leaves[4].append(va_s.reshape(db, t, cfg.n_kv, hd))
        leaves[5].append(ki_s.reshape(db, t, cfg.idx_dim))
        leaves[6].append(v_s.reshape(db, t, cfg.d_a))
    return (xp.reshape(nb, s, d), xs.reshape(db, t, d)) + tuple(jnp.stack(v) for v in leaves)


def kernel(x_prompt, x_sample, c_prompt, c_sample, cache_k, cache_v, cache_idx_k, page_table, mod_w, mod_b, norm_g, ffn_w_in, ffn_w_out, w_in, sgu_norm_g, sgu_w, sgu_b, q_norm_g, k_norm_g, w_branch_a, w_branch_b, w_out):
    return _forward(CFG, x_prompt, x_sample, c_prompt, c_sample, cache_k, cache_v, cache_idx_k, page_table,
                    mod_w, mod_b, norm_g, ffn_w_in, ffn_w_out, w_in, sgu_norm_g, sgu_w, sgu_b,
                    q_norm_g, k_norm_g, w_branch_a, w_branch_b, w_out)
```

```python
import collections
import functools

import jax
import jax.numpy as jnp
import numpy as np
from jax import lax
from jax.experimental import pallas as pl
from jax.experimental.pallas import tpu as pltpu

F32 = jnp.float32
BF16 = jnp.bfloat16
I32 = jnp.int32

LANES = 128
INT_MIN = -(2 ** 31)
NEG_BIG = -1e30
LOG2_E = 1.4426950408889634
VMEM_LIMIT = 56 * 1024 * 1024

Config = collections.namedtuple(
    "Config",
    "d_model seq depth dec_seq past_len page_size chunk d_a g_a n_heads head_dim n_kv "
    "n_idx_heads idx_dim topk_max rope_theta d_ff n_sub eps")

CFG = Config(d_model=1024, seq=8192, depth=2, dec_seq=4, past_len=16384, page_size=128, chunk=128,
             d_a=512, g_a=8, n_heads=8, head_dim=64, n_kv=4, n_idx_heads=8, idx_dim=64,
             topk_max=256, rope_theta=500000.0, d_ff=2816, n_sub=3, eps=1e-6)


def _dot(a, b):
    return lax.dot_general(a, b, (((1,), (0,)), ((), ())), preferred_element_type=F32)


def _dot_nt(a, b):
    return lax.dot_general(a, b, (((1,), (1,)), ((), ())), preferred_element_type=F32)


def _const_spec(shape):
    nd = len(shape)
    return pl.BlockSpec(shape, lambda *_: (0,) * nd, pipeline_mode=pl.Buffered(1))


def _params(n_grid):
    return pltpu.CompilerParams(dimension_semantics=("arbitrary",) * n_grid,
                                vmem_limit_bytes=VMEM_LIMIT)


def _mod_kernel(c_ref, w_ref, b_ref, o_ref):
    c = c_ref[...]
    s = (c * jax.nn.sigmoid(c)).astype(BF16)
    o_ref[0] = _dot(s, w_ref[0]) + b_ref[0]


def _modulation(c_all, mod_w, mod_b, tn=1152):
    depth, d, n = mod_w.shape
    r = c_all.shape[0]
    return pl.pallas_call(
        _mod_kernel,
        grid=(depth, n // tn),
        in_specs=[pl.BlockSpec((r, d), lambda l, j: (0, 0)),
                  pl.BlockSpec((1, d, tn), lambda l, j: (l, 0, j)),
                  pl.BlockSpec((1, 1, tn), lambda l, j: (l, 0, j))],
        out_specs=pl.BlockSpec((1, r, tn), lambda l, j: (l, 0, j)),
        out_shape=jax.ShapeDtypeStruct((depth, r, n), F32),
        compiler_params=_params(2),
        name="modulation",
    )(c_all, mod_w.astype(BF16), mod_b.reshape(depth, 1, n))


def _mod_row(mod_ref, idx, per_row):
    if per_row:
        return mod_ref[idx]
    return mod_ref[0, idx:idx + 1, :]


def _adaln(x, g, shift, scale, eps):
    y = x * lax.rsqrt(jnp.mean(x * x, axis=-1, keepdims=True) + eps)
    return (y * g) * (1.0 + scale) + shift


def _swiglu(h, wa_ref, wb_ref, w2_ref, n_chunks):
    d_ff = wa_ref.shape[1]
    cw = d_ff // n_chunks
    acc = None
    for c in range(n_chunks):
        a = _dot(h, wa_ref[:, c * cw:(c + 1) * cw])
        b = _dot(h, wb_ref[:, c * cw:(c + 1) * cw])
        t = ((a * jax.nn.sigmoid(a)) * b).astype(BF16)
        part = _dot(t, w2_ref[c * cw:(c + 1) * cw, :])
        acc = part if acc is None else acc + part
    return acc


def _ffn_kernel(*refs, merge, per_row, sub, n_chunks, eps):
    if merge:
        (x_ref, mod_ref, g_ref, ya_ref, gb_ref, ob_ref, pb_ref, wo_ref,
         wa_ref, wb_ref, w2_ref, o_ref) = refs
    else:
        x_ref, mod_ref, g_ref, wa_ref, wb_ref, w2_ref, o_ref = refs
    x = x_ref[...]
    if merge:
        y = ya_ref[...] + gb_ref[...] * _dot(ob_ref[...], pb_ref[...])
        x = x + _mod_row(mod_ref, 5, per_row) * _dot(y.astype(BF16), wo_ref[...])
    shift = _mod_row(mod_ref, 3 * sub, per_row)
    scale = _mod_row(mod_ref, 3 * sub + 1, per_row)
    gate = _mod_row(mod_ref, 3 * sub + 2, per_row)
    h = _adaln(x, g_ref[...], shift, scale, eps).astype(BF16)
    o_ref[...] = x + (0.5 * gate) * _swiglu(h, wa_ref, wb_ref, w2_ref, n_chunks)


def _ffn(x, mod, g, wa, wb, w2, cfg, *, sub, tm, rows_per_mod, merge_args=None):
    n, d = x.shape
    per_row = rows_per_mod is None
    merge = merge_args is not None
    row = lambda i: (i, 0)
    if per_row:
        mod_spec = pl.BlockSpec((mod.shape[0], tm, d), lambda i: (0, i, 0))
    else:
        tiles_per_mod = rows_per_mod // tm
        mod_spec = pl.BlockSpec((1, mod.shape[1], d), lambda i: (i // tiles_per_mod, 0, 0))
    in_specs = [pl.BlockSpec((tm, d), row), mod_spec, _const_spec((1, d))]
    args = [x, mod, g.reshape(1, d)]
    if merge:
        ya, gb, ob, pb, wo = merge_args
        in_specs += [pl.BlockSpec((tm, d), row), pl.BlockSpec((tm, d), row),
                     pl.BlockSpec((tm, ob.shape[1]), row), _const_spec(pb.shape), _const_spec(wo.shape)]
        args += [ya, gb, ob, pb, wo]
    in_specs += [_const_spec(wa.shape), _const_spec(wb.shape), _const_spec(w2.shape)]
    args += [wa, wb, w2]
    kern = functools.partial(_ffn_kernel, merge=merge, per_row=per_row, sub=sub, n_chunks=2, eps=cfg.eps)
    return pl.pallas_call(
        kern,
        grid=(n // tm,),
        in_specs=in_specs,
        out_specs=pl.BlockSpec((tm, d), row),
        out_shape=jax.ShapeDtypeStruct((n, d), F32),
        compiler_params=_params(1),
        name="merge_ffn" if merge else "ffn",
    )(*args)


def _gelu_tanh(x):
    return 0.5 * x * (1.0 + jnp.tanh(np.sqrt(2.0 / np.pi).astype(np.float32) * (x + 0.044715 * (x * x * x))))


def _head_mean(sq, bd_ref, head_dim):
    hi = sq.astype(BF16)
    lo = (sq - hi.astype(F32)).astype(BF16)
    return (_dot(hi, bd_ref[...]) + _dot(lo, bd_ref[...])) * (1.0 / head_dim)


def _rope(x, cos, sin_up, sin_dn, shift):
    parts = []
    for j in range(x.shape[1] // LANES):
        xs = x[:, j * LANES:(j + 1) * LANES]
        parts.append(xs * cos + pltpu.roll(xs, shift, 1) * sin_up + pltpu.roll(xs, LANES - shift, 1) * sin_dn)
    return parts[0] if len(parts) == 1 else jnp.concatenate(parts, axis=1)


def _proj_kernel(*refs, per_row, sample, cfg):
    (x_ref, mod_ref, g_ref, w1_ref, w2_ref, w3_ref, sg_ref, wm_ref, sb_ref, qg_ref, kg_ref,
     bdq_ref, bdk_ref, cos_ref, sup_ref, sdn_ref, pa_ref) = refs[:17]
    outs = refs[17:]
    eps, hd, d_a = cfg.eps, cfg.head_dim, cfg.d_a
    nq, nkv, nqi = cfg.n_heads * hd, cfg.n_kv * hd, cfg.n_idx_heads * cfg.idx_dim
    tm = x_ref.shape[0]

    x = x_ref[...]
    h = _adaln(x, g_ref[...], _mod_row(mod_ref, 3, per_row), _mod_row(mod_ref, 4, per_row), eps).astype(BF16)
    z1 = _dot(h, w1_ref[...])
    z2 = _dot(h, w2_ref[...])
    z3 = _dot(h, w3_ref[...])
    o = 0
    zu = z1[:, o:o + d_a]; o += d_a
    zv = z1[:, o:o + d_a]; o += d_a
    zq = z1[:, o:o + nq]; o += nq
    zk = z1[:, o:o + nkv]; o += nkv
    zva = z1[:, o:o + nkv]; o += nkv
    zqi = z1[:, o:o + nqi]

    u = _gelu_tanh(zu)
    gv = _gelu_tanh(zv)
    v = gv * lax.rsqrt(jnp.mean(gv * gv, axis=-1, keepdims=True) + eps) * sg_ref[...]
    vb = v.astype(BF16)
    ck = wm_ref.shape[1]
    rr = lax.broadcasted_iota(I32, (ck, ck), 0)
    cc = lax.broadcasted_iota(I32, (ck, ck), 1)
    lane_group = lax.broadcasted_iota(I32, (ck, d_a), 1) // (d_a // cfg.g_a)
    wms = [jnp.where(rr >= cc, wm_ref[g], 0.0).astype(BF16) for g in range(cfg.g_a)]
    mixed_chunks = []
    for c in range(tm // ck):
        vc = vb[c * ck:(c + 1) * ck, :]
        mixed = sb_ref[...]
        for g in range(cfg.g_a):
            mixed = mixed + jnp.where(lane_group == g, _dot(wms[g], vc), 0.0)
        mixed_chunks.append(mixed)
    mixed = mixed_chunks[0] if len(mixed_chunks) == 1 else jnp.concatenate(mixed_chunks, axis=0)
    o_a = (u * mixed).astype(BF16)
    ga = jax.nn.sigmoid(z3[:, :cfg.d_model])
    gb = jax.nn.sigmoid(z3[:, cfg.d_model:])
    ya = ga * _dot(o_a, pa_ref[...])

    cos, sup, sdn = cos_ref[...], sup_ref[...], sdn_ref[...]
    half = hd // 8
    q = zq * lax.rsqrt(_head_mean(zq * zq, bdq_ref, hd) + eps) * qg_ref[...]
    q = _rope(q, cos, sup, sdn, half)
    k = zk * lax.rsqrt(_head_mean(zk * zk, bdk_ref, hd) + eps) * kg_ref[...]
    k = _rope(k, cos, sup, sdn, half)
    qi = _rope(zqi, cos, sup, sdn, half)
    ki = _rope(z2, cos, sup, sdn, half)[:, :cfg.idx_dim]
    wi_scale = cfg.n_idx_heads ** -0.5

    if sample:
        (k_ref, va_ref, ki_ref, v_ref, q_ref, qi_ref, wi_ref, ya_ref, gb_ref) = outs
        v_ref[...] = v
        q_ref[...] = q
        qi_ref[...] = qi
        wi_ref[...] = z2[:, cfg.idx_dim:cfg.idx_dim + cfg.n_idx_heads] * wi_scale
    else:
        (k_ref, va_ref, ki_ref, qt_ref, qit_ref, kh_ref, vt_ref, kib_ref, wi_ref, ya_ref, gb_ref) = outs
        qt_ref[0] = (q * (LOG2_E * hd ** -0.5)).T.astype(BF16)
        qit_ref[0] = qi.T.astype(BF16)
        vt_ref[0] = zva.T.astype(BF16)
        for hh in range(cfg.n_kv):
            kh_ref[0, hh] = k[:, hh * hd:(hh + 1) * hd].astype(BF16)
        kib_ref[0] = ki.astype(BF16)
        wi_ref[0] = z2.T[cfg.idx_dim:cfg.idx_dim + cfg.n_idx_heads, :] * wi_scale
    k_ref[...] = k
    va_ref[...] = zva
    ki_ref[...] = ki
    ya_ref[...] = ya
    gb_ref[...] = gb


def _rope_tables(pos, cfg):
    hd = cfg.head_dim
    rd = hd // 4
    half = rd // 2
    freqs = cfg.rope_theta ** (-jnp.arange(half, dtype=F32) * 2.0 / rd)
    ang = pos.astype(F32)[:, None] * freqs[None, :]
    cos, sin = jnp.cos(ang), jnp.sin(ang)
    n = pos.shape[0]
    one = jnp.ones((n, hd - rd), F32)
    zero = jnp.zeros((n, hd - rd), F32)
    zh = jnp.zeros((n, half), F32)
    cos_h = jnp.concatenate([cos, cos, one], axis=1)
    sup_h = jnp.concatenate([zh, sin, zero], axis=1)
    sdn_h = jnp.concatenate([-sin, zh, zero], axis=1)
    rep = LANES // hd
    return tuple(jnp.tile(t, (1, rep)) for t in (cos_h, sup_h, sdn_h))


def _proj_weights(lw, cfg):
    d_a, hd = cfg.d_a, cfg.head_dim
    nq, nkv, nqi = cfg.n_heads * hd, cfg.n_kv * hd, cfg.n_idx_heads * cfg.idx_dim
    n1 = 2 * d_a + nq + 2 * nkv + nqi
    n2 = cfg.idx_dim + cfg.n_idx_heads
    w_in = lw["w_in"]
    w1 = w_in[:, :n1].astype(BF16)
    w2 = jnp.pad(w_in[:, n1:n1 + n2], ((0, 0), (0, LANES - n2))).astype(BF16)
    w3 = w_in[:, n1 + n2:].astype(BF16)
    gd = d_a // cfg.g_a

    def block_diag(width):
        hidx = jnp.arange(width) // hd
        return (hidx[:, None] == hidx[None, :]).astype(BF16)

    return dict(
        w1=w1, w2=w2, w3=w3,
        sg=lw["sgu_norm_g"].reshape(1, d_a),
        qg=jnp.tile(lw["q_norm_g"], cfg.n_heads).reshape(1, nq),
        kg=jnp.tile(lw["k_norm_g"], cfg.n_kv).reshape(1, nkv),
        bdq=block_diag(nq), bdk=block_diag(nkv),
        pa=lw["w_branch_a"].astype(BF16),
        wm_p=lw["sgu_w"],
        sb_p=jnp.repeat(lw["sgu_b"].T, gd, axis=1),
    )


def _sample_gate_weights(lw, cfg, n_seq):
    t = cfg.dec_seq
    gd = cfg.d_a // cfg.g_a
    eye = jnp.eye(n_seq, dtype=F32)
    wm = jax.vmap(lambda w: jnp.kron(eye, w))(lw["sgu_w"][:, :t, :t])
    sb = jnp.tile(jnp.repeat(lw["sgu_b"][:, :t].T, gd, axis=1), (n_seq, 1))
    return wm, sb


def _proj(x, mod, g, pw, wm, sb, tabs, cfg, *, tm, rows_per_mod, sample, n_batch=None):
    n, d = x.shape
    per_row = rows_per_mod is None
    row = lambda i: (i, 0)
    if per_row:
        mod_spec = pl.BlockSpec((mod.shape[0], tm, d), lambda i: (0, i, 0))
        tab_spec = pl.BlockSpec((tm, LANES), row)
    else:
        tiles_per_mod = rows_per_mod // tm
        mod_spec = pl.BlockSpec((1, mod.shape[1], d), lambda i: (i // tiles_per_mod, 0, 0))
        tab_spec = pl.BlockSpec((tm, LANES), lambda i: (i % tiles_per_mod, 0))
    consts = [g.reshape(1, d), pw["w1"], pw["w2"], pw["w3"], pw["sg"], wm, sb, pw["qg"], pw["kg"],
              pw["bdq"], pw["bdk"]]
    in_specs = ([pl.BlockSpec((tm, d), row), mod_spec] + [_const_spec(a.shape) for a in consts]
                + [tab_spec] * 3 + [_const_spec(pw["pa"].shape)])
    args = [x, mod] + consts + list(tabs) + [pw["pa"]]
    hd = cfg.head_dim
    nq, nkv, nqi = cfg.n_heads * hd, cfg.n_kv * hd, cfg.n_idx_heads * cfg.idx_dim

    def rows(width, dtype=F32):
        return pl.BlockSpec((tm, width), row), jax.ShapeDtypeStruct((n, width), dtype)

    outs = [rows(nkv), rows(nkv), rows(cfg.idx_dim)]
    if sample:
        outs += [rows(cfg.d_a), rows(nq), rows(nqi), rows(cfg.n_idx_heads)]
    else:
        s = rows_per_mod
        tps = s // tm

        def transposed(width, dtype=BF16):
            return (pl.BlockSpec((1, width, tm), lambda i: (i // tps, 0, i % tps)),
                    jax.ShapeDtypeStruct((n_batch, width, s), dtype))

        outs += [transposed(nq), transposed(nqi),
                 (pl.BlockSpec((1, cfg.n_kv, tm, hd), lambda i: (i // tps, 0, i % tps, 0)),
                  jax.ShapeDtypeStruct((n_batch, cfg.n_kv, s, hd), BF16)),
                 transposed(nkv),
                 (pl.BlockSpec((1, tm, cfg.idx_dim), lambda i: (i // tps, i % tps, 0)),
                  jax.ShapeDtypeStruct((n_batch, s, cfg.idx_dim), BF16)),
                 transposed(cfg.n_idx_heads, F32)]
    outs += [rows(d), rows(d)]
    kern = functools.partial(_proj_kernel, per_row=per_row, sample=sample, cfg=cfg)
    return pl.pallas_call(
        kern,
        grid=(n // tm,),
        in_specs=in_specs,
        out_specs=[o[0] for o in outs],
        out_shape=[o[1] for o in outs],
        compiler_params=_params(1),
        name="proj_sample" if sample else "proj_prompt",
    )(*args)


def _score_key(x):
    b = lax.bitcast_convert_type(x + 0.0, I32)
    return b ^ ((b >> 31) & 0x7FFFFFFF)


SUBLANES = 8
BF16_ROWS = 16


class _KeyBlock:
    def __init__(self, ref, key_axis, tk, tiles_per_trip=1):
        self.ref, self.key_axis, self.tk, self.tiles_per_trip = ref, key_axis, tk, tiles_per_trip
        self.n_vec = ref.shape[1 - key_axis]
        if key_axis == 1:
            self.unit = (self.n_vec, LANES)
            self.pos = lax.broadcasted_iota(I32, self.unit, 1)
        else:
            self.unit = (SUBLANES, self.n_vec)
            self.pos = lax.broadcasted_iota(I32, (tk, self.n_vec), 0).reshape(
                tk // SUBLANES, SUBLANES, self.n_vec)

    def bcast(self, a):
        return jnp.broadcast_to(a, self.unit)

    def tile(self, off):
        if self.key_axis == 1:
            x = self.ref[:, pl.ds(off, self.tk)]
            return [(x[:, c * LANES:(c + 1) * LANES], self.pos + (off + c * LANES))
                    for c in range(self.tk // LANES)]
        x = self.ref[pl.ds(off, self.tk), :]
        return [(x.reshape(self.tk // SUBLANES, SUBLANES, self.n_vec), self.pos + off)]

    def count(self, n_tiles, pred):
        group = self.tiles_per_trip

        def body(j, acc):
            for u in range(group):
                for x, pos in self.tile(pl.multiple_of((j * group + u) * self.tk, self.tk)):
                    hits = jnp.where(pred(x, pos), 1, 0)
                    acc = acc + (hits if self.key_axis == 1 else jnp.sum(hits, axis=0))
            return acc

        n_trips = n_tiles if group == 1 else (n_tiles + group - 1) // group
        acc = lax.fori_loop(0, n_trips, body, jnp.zeros(self.unit, I32))
        return jnp.sum(acc.astype(F32), axis=self.key_axis, keepdims=True).astype(I32)


def _select_threshold(kb, n_tiles, k_eff, live=None):
    def count_ge(cand):
        cb = kb.bcast(cand)
        return kb.count(n_tiles, lambda x, _: x >= cb)

    zero = jnp.zeros_like(k_eff)
    c0 = count_ge(zero)
    c1 = count_ge(zero + 1)
    t0 = jnp.where(c0 >= k_eff, 0, INT_MIN).astype(I32)
    zero_final = jnp.where(c0 >= k_eff, jnp.where(c1 < k_eff, 1, 0), 0).astype(I32)
    tied0 = jnp.where(c0 > k_eff, zero_final, 0)
    open0 = 1 - zero_final
    if live is not None:
        open0 = jnp.where(live, open0, 0)
        tied0 = jnp.where(live, tied0, 0)

    def any_set(flags):
        return jnp.max(flags.astype(F32)) > 0.5

    def cond(state):
        it, _, opn = state
        return jnp.logical_and(it < 31, any_set(opn))

    def body(state):
        it, t, opn = state
        cand = t + lax.shift_left(jnp.int32(1), 30 - it)
        c = count_ge(cand)
        t = jnp.where(opn > 0, jnp.where(c >= k_eff, cand, t), t)
        opn = jnp.where(c == k_eff, 0, opn)
        return it + 1, t, opn

    _, t, opn = lax.while_loop(cond, body, (jnp.int32(0), t0, open0))
    return t, any_set(opn + tied0)


def _tie_need(kb, n_tiles, k_eff, t):
    tb = kb.bcast(t)
    return (k_eff - kb.count(n_tiles, lambda x, _: x > tb)).astype(F32)


def _selection_bias(x, t, tie=None):
    if tie is None:
        return jnp.where(x >= t, 0.0, NEG_BIG)
    prefix, need = tie
    return jnp.where(x > t, 0.0, jnp.where(x < t, NEG_BIG, jnp.where(prefix <= need, 0.0, NEG_BIG)))


def _tie_indicator(x, t):
    return jnp.where(x == t, 1.0, 0.0).astype(BF16)


def _softmax_step(s, pv, m_ref, l_ref, acc_ref, idx, key_axis, exp=jnp.exp):
    m_old = m_ref[idx]
    m_new = jnp.maximum(m_old, jnp.max(s, axis=key_axis, keepdims=True))
    p = exp(s - m_new)
    alpha = exp(m_old - m_new)
    l_ref[idx] = alpha * l_ref[idx] + jnp.sum(p, axis=key_axis, keepdims=True)
    acc_ref[idx] = alpha * acc_ref[idx] + pv(p.astype(BF16))
    m_ref[idx] = m_new


def _prompt_attn_kernel(qit_ref, wt_ref, ki_ref, qt_ref, k_ref, vt_ref, o_ref,
                        sc_ref, bias_ref, s_ref, m_ref, acc_ref, *, qb, topk, cfg):
    i = pl.program_id(1)
    tk = qb
    n_tiles = i + 1
    q_pos = i * qb + lax.broadcasted_iota(I32, (1, qb), 1)
    key_pos = lax.broadcasted_iota(I32, (tk, qb), 0)
    wt = wt_ref[0]
    rep = cfg.n_heads // cfg.n_kv
    hd, di = cfg.head_dim, cfg.idx_dim

    def score_tile(j, carry):
        off = pl.multiple_of(j * tk, tk)
        kib = ki_ref[0, pl.ds(off, tk), :]
        acc = jnp.zeros((tk, qb), F32)
        for h in range(cfg.n_idx_heads):
            acc = acc + wt[h:h + 1, :] * jnp.maximum(_dot(kib, qit_ref[0, h * di:(h + 1) * di, :]), 0.0)
        sc_ref[pl.ds(off, tk), :] = jnp.where(key_pos + off <= q_pos, _score_key(acc), INT_MIN)
        return carry

    lax.fori_loop(0, n_tiles, score_tile, 0)

    pair = 2 if (sc_ref.shape[0] // tk) % 2 == 0 else 1
    if pair == 2:
        @pl.when(n_tiles % 2 == 1)
        def _():
            sc_ref[pl.ds(pl.multiple_of(n_tiles * tk, tk), tk), :] = jnp.full((tk, qb), INT_MIN, I32)

    kb = _KeyBlock(sc_ref, 0, tk, tiles_per_trip=pair)
    k_eff = jnp.minimum(q_pos + 1, topk)
    t, has_ties = _select_threshold(kb, n_tiles, k_eff)
    need = lax.cond(has_ties, lambda: _tie_need(kb, n_tiles, k_eff, t), lambda: jnp.zeros((1, qb), F32))

    m_ref[...] = jnp.full(m_ref.shape, NEG_BIG, F32)
    acc_ref[...] = jnp.zeros(acc_ref.shape, F32)
    ones_rows = jnp.where(lax.broadcasted_iota(I32, (BF16_ROWS, tk), 0) == 0, 1.0, 0.0).astype(BF16)
    prefix_mat = jnp.where(key_pos[:, :tk] >= lax.broadcasted_iota(I32, (tk, tk), 1), 1.0, 0.0).astype(BF16)

    def attend_tile(j, tied_before):
        off = pl.multiple_of(j * tk, tk)
        x = sc_ref[pl.ds(off, tk), :]

        def tie_bias(tied_before):
            prefix = _dot(prefix_mat, _tie_indicator(x, t)) + tied_before
            bias_ref[...] = _selection_bias(x, t, (prefix, need))
            return prefix[tk - 1:tk, :]

        def plain_bias(tied_before):
            bias_ref[...] = _selection_bias(x, t)
            return tied_before

        tied_before = lax.cond(has_ties, tie_bias, plain_bias, tied_before)
        bias = jnp.concatenate([bias_ref[...]] * rep, axis=1)
        for g in range(cfg.n_kv):
            qt = jnp.concatenate([qt_ref[0, h * hd:(h + 1) * hd, :] for h in range(g * rep, (g + 1) * rep)],
                                 axis=1)
            s_ref[g] = _dot(k_ref[0, g, pl.ds(off, tk), :], qt) + bias
        for g in range(cfg.n_kv):
            vt = jnp.concatenate([vt_ref[0, g * hd:(g + 1) * hd, pl.ds(off, tk)], ones_rows], axis=0)
            s = s_ref[g]
            m_old = m_ref[g]
            m_new = jnp.maximum(m_old, jnp.max(s, axis=0, keepdims=True))
            p = jnp.exp2(s - m_new).astype(BF16)
            acc_ref[g] = jnp.exp2(m_old - m_new) * acc_ref[g] + _dot(vt, p)
            m_ref[g] = m_new
        return tied_before

    lax.fori_loop(0, n_tiles, attend_tile, jnp.zeros((1, qb), F32))

    heads = []
    for h in range(cfg.n_heads):
        g, r = divmod(h, rep)
        a = acc_ref[g, :, r * qb:(r + 1) * qb]
        heads.append(a[:hd] / a[hd:hd + 1])
    o_ref[0] = jnp.concatenate(heads, axis=0).T.astype(BF16)


def _prompt_attention(qit, wt, kib, qt, kh, vt, cfg, *, qb=256):
    nb, _, s, hd = kh.shape
    qb = min(qb, s)
    topk = min(cfg.topk_max, s // 4)
    rep = cfg.n_heads // cfg.n_kv
    kern = functools.partial(_prompt_attn_kernel, qb=qb, topk=topk, cfg=cfg)
    q_block = lambda b, i: (b, 0, i)

    def per_batch(shape):
        nd = len(shape)
        return pl.BlockSpec((1,) + shape, lambda b, i: (b,) + (0,) * nd, pipeline_mode=pl.Buffered(1))

    return pl.pallas_call(
        kern,
        grid=(nb, s // qb),
        in_specs=[pl.BlockSpec((1, cfg.n_idx_heads * cfg.idx_dim, qb), q_block),
                  pl.BlockSpec((1, cfg.n_idx_heads, qb), q_block),
                  per_batch((s, cfg.idx_dim)),
                  pl.BlockSpec((1, cfg.n_heads * hd, qb), q_block),
                  per_batch((cfg.n_kv, s, hd)),
                  per_batch((cfg.n_kv * hd, s))],
        out_specs=pl.BlockSpec((1, qb, cfg.n_heads * hd), lambda b, i: (b, i, 0)),
        out_shape=jax.ShapeDtypeStruct((nb, s, cfg.n_heads * hd), BF16),
        scratch_shapes=[pltpu.VMEM((s, qb), I32),
                        pltpu.VMEM((qb, qb), F32),
                        pltpu.VMEM((cfg.n_kv, qb, rep * qb), F32),
                        pltpu.VMEM((cfg.n_kv, 1, rep * qb), F32),
                        pltpu.VMEM((cfg.n_kv, hd + BF16_ROWS, rep * qb), F32)],
        compiler_params=_params(2),
        name="prompt_attention",
    )(qit, wt, kib, qt, kh, vt)


TOK_PAD = 8
PAGES_PER_STEP = 16


def _samp_index_kernel(pt_ref, qi_ref, w_ref, kit_ref, *refs, n_pages_step, tk, topk, cfg):
    page_refs = refs[:n_pages_step]
    bias_ref, sc_ref = refs[n_pages_step:]
    c = pl.program_id(1)
    n_chunks = pl.num_programs(1)
    nh = cfg.n_idx_heads
    qi = qi_ref[0]
    w = w_ref[0]
    width = n_pages_step * cfg.page_size
    total = sc_ref.shape[1]

    def token_scores(keys_t):
        y = w * jnp.maximum(_dot(qi, keys_t), 0.0)
        return jnp.sum(y.reshape(nh, TOK_PAD, y.shape[1]), axis=0)

    kc = jnp.concatenate([r[...] for r in page_refs], axis=1).astype(BF16)
    sc_ref[:, pl.ds(pl.multiple_of(c * width, width), width)] = _score_key(token_scores(kc))

    @pl.when(c == n_chunks - 1)
    def _():
        tail = kit_ref.shape[2]
        tok = lax.broadcasted_iota(I32, (TOK_PAD, tail), 0)
        new = lax.broadcasted_iota(I32, (TOK_PAD, tail), 1)
        valid = new <= jnp.minimum(tok, cfg.dec_seq - 1)
        sc_ref[:, total - tail:] = jnp.where(valid, _score_key(token_scores(kit_ref[0])), INT_MIN)
        tok1 = lax.broadcasted_iota(I32, (TOK_PAD, 1), 0)
        k_eff = jnp.minimum(cfg.past_len + jnp.minimum(tok1, cfg.dec_seq - 1) + 1, topk)
        kb = _KeyBlock(sc_ref, 1, tk, tiles_per_trip=total // tk)
        t, _ = _select_threshold(kb, total // tk, k_eff, live=tok1 < cfg.dec_seq)
        need = _tie_need(kb, total // tk, k_eff, t)
        prefix_mat = jnp.where(lax.broadcasted_iota(I32, (LANES, LANES), 0)
                               <= lax.broadcasted_iota(I32, (LANES, LANES), 1), 1.0, 0.0).astype(BF16)

        tied_before = jnp.zeros((TOK_PAD, 1), F32)
        for j in range(total // LANES):
            x = sc_ref[:, j * LANES:(j + 1) * LANES]
            tied = _tie_indicator(x, t)
            prefix = _dot(tied, prefix_mat) + tied_before
            bias_ref[0, :, j * LANES:(j + 1) * LANES] = _selection_bias(x, t, (prefix, need))
            tied_before = tied_before + jnp.sum(tied.astype(F32), axis=1, keepdims=True)


def _samp_attn_kernel(pt_ref, q_ref, bias_ref, biast_ref, kt_ref, vt_ref, *refs, n_pages_step, cfg):
    k_pages = refs[:n_pages_step]
    v_pages = refs[n_pages_step:2 * n_pages_step]
    o_ref, m_ref, l_ref, acc_ref = refs[2 * n_pages_step:]
    c = pl.program_id(1)
    n_chunks = pl.num_programs(1)
    q = q_ref[0]
    hd = cfg.head_dim
    rep = cfg.n_heads // cfg.n_kv

    @pl.when(c == 0)
    def _():
        m_ref[...] = jnp.full(m_ref.shape, NEG_BIG, F32)
        l_ref[...] = jnp.zeros(l_ref.shape, F32)
        acc_ref[...] = jnp.zeros(acc_ref.shape, F32)

    def step(kt, vt, bias):
        s = _dot(q, kt) + jnp.concatenate([bias] * cfg.n_heads, axis=0)
        _softmax_step(s, lambda p: _dot_nt(p, vt), m_ref, l_ref, acc_ref, 0, 1)

    kc = jnp.concatenate([r[...] for r in k_pages], axis=1).astype(BF16)
    vc = jnp.concatenate([r[...] for r in v_pages], axis=1).astype(BF16)
    step(kc, vc, bias_ref[0])

    @pl.when(c == n_chunks - 1)
    def _():
        step(kt_ref[0], vt_ref[0], biast_ref[0])
        o = acc_ref[0] / l_ref[0]
        for h in range(cfg.n_heads):
            g = h // rep
            o_ref[0, h] = o[h * TOK_PAD:(h + 1) * TOK_PAD, g * hd:(g + 1) * hd]


def _sample_attention(l, q, qi, wi, k_new, v_new, ki_new, cache_k, cache_v, cache_idx_k, page_table, cfg):
    db, n_pages = page_table.shape
    t, hd, ps = cfg.dec_seq, cfg.head_dim, cfg.page_size
    nh, nkv, nih, di = cfg.n_heads, cfg.n_kv, cfg.n_idx_heads, cfg.idx_dim
    rep = nh // nkv
    pstep = min(PAGES_PER_STEP, n_pages)
    n_chunks = n_pages // pstep
    width = pstep * ps
    tail = LANES
    total = cfg.past_len + tail
    topk = min(cfg.topk_max, (cfg.past_len + t) // 4)
    n_lane_groups = total // LANES
    tk = LANES * max(d for d in range(1, 49) if n_lane_groups % d == 0)

    def head_major(a, nheads, dim):
        a = a.reshape(db, t, nheads, dim).transpose(0, 2, 1, 3)
        return jnp.pad(a, ((0, 0), (0, 0), (0, TOK_PAD - t), (0, 0)))

    qi_rows = head_major(qi, nih, di).reshape(db, nih * TOK_PAD, di).astype(BF16)
    w_rows = head_major(wi, nih, 1).reshape(db, nih * TOK_PAD, 1)
    group = (jnp.arange(nh) // rep)[:, None] == jnp.arange(nkv)[None, :]
    q_rows = head_major(q * (hd ** -0.5), nh, hd)[:, :, :, None, :] * group[None, :, None, :, None].astype(F32)
    q_rows = q_rows.reshape(db, nh * TOK_PAD, nkv * hd).astype(BF16)

    def tail_pad(a):
        a = a.reshape(db, t, a.shape[-1]).transpose(0, 2, 1)
        return jnp.pad(a, ((0, 0), (0, 0), (0, tail - t))).astype(BF16)

    ki_tail, k_tail, v_tail = tail_pad(ki_new), tail_pad(k_new), tail_pad(v_new)
    ck = cache_k.transpose(0, 1, 3, 4, 2).reshape(cache_k.shape[:2] + (nkv * hd, ps))
    cv = cache_v.transpose(0, 1, 3, 4, 2).reshape(cache_v.shape[:2] + (nkv * hd, ps))
    cik = cache_idx_k.transpose(0, 1, 3, 2)

    def page_spec(j, w):
        return pl.BlockSpec((None, None, w, ps), lambda b, c, pt: (l, pt[b, c * pstep + j], 0, 0))

    per_seq3 = lambda b, c, pt: (b, 0, 0)
    bias = pl.pallas_call(
        functools.partial(_samp_index_kernel, n_pages_step=pstep, tk=tk, topk=topk, cfg=cfg),
        grid_spec=pltpu.PrefetchScalarGridSpec(
            num_scalar_prefetch=1,
            grid=(db, n_chunks),
            in_specs=[pl.BlockSpec((1, nih * TOK_PAD, di), per_seq3),
                      pl.BlockSpec((1, nih * TOK_PAD, 1), per_seq3),
                      pl.BlockSpec((1, di, tail), per_seq3)]
                     + [page_spec(j, di) for j in range(pstep)],
            out_specs=pl.BlockSpec((1, TOK_PAD, total), per_seq3),
            scratch_shapes=[pltpu.VMEM((TOK_PAD, total), I32)]),
        out_shape=jax.ShapeDtypeStruct((db, TOK_PAD, total), F32),
        compiler_params=_params(2),
        name="sample_index",
    )(page_table, qi_rows, w_rows, ki_tail, *([cik] * pstep))

    out = pl.pallas_call(
        functools.partial(_samp_attn_kernel, n_pages_step=pstep, cfg=cfg),
        grid_spec=pltpu.PrefetchScalarGridSpec(
            num_scalar_prefetch=1,
            grid=(db, n_chunks),
            in_specs=[pl.BlockSpec((1, nh * TOK_PAD, nkv * hd), per_seq3),
                      pl.BlockSpec((1, TOK_PAD, width), lambda b, c, pt: (b, 0, c)),
                      pl.BlockSpec((1, TOK_PAD, tail), lambda b, c, pt: (b, 0, total // tail - 1)),
                      pl.BlockSpec((1, nkv * hd, tail), per_seq3),
                      pl.BlockSpec((1, nkv * hd, tail), per_seq3)]
                     + [page_spec(j, nkv * hd) for j in range(pstep)] * 2,
            out_specs=pl.BlockSpec((1, nh, TOK_PAD, hd), lambda b, c, pt: (b, 0, 0, 0)),
            scratch_shapes=[pltpu.VMEM((1, nh * TOK_PAD, 1), F32),
                            pltpu.VMEM((1, nh * TOK_PAD, 1), F32),
                            pltpu.VMEM((1, nh * TOK_PAD, nkv * hd), F32)]),
        out_shape=jax.ShapeDtypeStruct((db, nh, TOK_PAD, hd), F32),
        compiler_params=_params(2),
        name="sample_attention",
    )(page_table, q_rows, bias, bias, k_tail, v_tail, *([ck] * pstep), *([cv] * pstep))
    return out[:, :, :t, :].transpose(0, 2, 1, 3).reshape(db * t, nh * hd).astype(BF16)


def _forward(cfg, x_prompt, x_sample, c_prompt, c_sample, cache_k, cache_v, cache_idx_k, page_table,
             mod_w, mod_b, norm_g, ffn_w_in, ffn_w_out, w_in, sgu_norm_g, sgu_w, sgu_b,
             q_norm_g, k_norm_g, w_branch_a, w_branch_b, w_out):
    nb, s, d = x_prompt.shape
    db, t, _ = x_sample.shape
    depth = mod_w.shape[0]
    hd = cfg.head_dim
    n_mod = cfg.n_sub * 3

    n_seq = nb + db
    r_pad = -(-n_seq // 8) * 8
    c_all = jnp.pad(jnp.concatenate([c_prompt, c_sample], axis=0), ((0, r_pad - n_seq), (0, 0)))
    mod = _modulation(c_all, mod_w, mod_b).reshape(depth, r_pad, n_mod, d)
    mod_p = mod[:, :nb]
    mod_s = jnp.repeat(mod[:, nb:n_seq], t, axis=1).transpose(0, 2, 1, 3)

    tabs_p = _rope_tables(jnp.arange(s), cfg)
    tabs_s = _rope_tables(jnp.tile(cfg.past_len + jnp.arange(t), db), cfg)

    xp = x_prompt.reshape(nb * s, d)
    xs = x_sample.reshape(db * t, d)
    tm_ffn = min(512, s)
    tm_proj = min(256, s)
    d_ff = ffn_w_out.shape[2]
    leaves = [[] for _ in range(7)]
    for l in range(depth):
        lw = dict(w_in=w_in[l], sgu_norm_g=sgu_norm_g[l], sgu_w=sgu_w[l], sgu_b=sgu_b[l],
                  q_norm_g=q_norm_g[l], k_norm_g=k_norm_g[l], w_branch_a=w_branch_a[l])
        pw = _proj_weights(lw, cfg)
        wm_s, sb_s = _sample_gate_weights(lw, cfg, db)
        ffn_w = [(ffn_w_in[l, j, :, :d_ff].astype(BF16), ffn_w_in[l, j, :, d_ff:].astype(BF16),
                  ffn_w_out[l, j].astype(BF16)) for j in range(2)]
        pb = w_branch_b[l].astype(BF16)
        wo = w_out[l].astype(BF16)

        xp = _ffn(xp, mod_p[l], norm_g[l, 0], *ffn_w[0], cfg, sub=0, tm=tm_ffn, rows_per_mod=s)
        (k_p, va_p, ki_p, qt, qit, kh, vt, kib, wt, ya_p, gb_p) = _proj(
            xp, mod_p[l], norm_g[l, 1], pw, pw["wm_p"], pw["sb_p"], tabs_p, cfg,
            tm=tm_proj, rows_per_mod=s, sample=False, n_batch=nb)
        ob_p = _prompt_attention(qit, wt, kib, qt, kh, vt, cfg)
        xp = _ffn(xp, mod_p[l], norm_g[l, 2], *ffn_w[1], cfg, sub=2, tm=tm_ffn, rows_per_mod=s,
                  merge_args=(ya_p, gb_p, ob_p.reshape(nb * s, -1), pb, wo))

        xs = _ffn(xs, mod_s[l], norm_g[l, 0], *ffn_w[0], cfg, sub=0, tm=db * t, rows_per_mod=None)
        (k_s, va_s, ki_s, v_s, q_s, qi_s, wi_s, ya_s, gb_s) = _proj(
            xs, mod_s[l], norm_g[l, 1], pw, wm_s, sb_s, tabs_s, cfg,
            tm=db * t, rows_per_mod=None, sample=True)
        ob_s = _sample_attention(l, q_s, qi_s, wi_s, k_s, va_s, ki_s, cache_k, cache_v, cache_idx_k,
                                 page_table, cfg)
        xs = _ffn(xs, mod_s[l], norm_g[l, 2], *ffn_w[1], cfg, sub=2, tm=db * t, rows_per_mod=None,
                  merge_args=(ya_s, gb_s, ob_s, pb, wo))

        leaves[0].append(k_p.reshape(nb, s, cfg.n_kv, hd))
        leaves[1].append(va_p.reshape(nb, s, cfg.n_kv, hd))
        leaves[2].append(ki_p.reshape(nb, s, cfg.idx_dim))
        leaves[3].append(k_s.reshape(db, t, cfg.n_kv, hd))
        leaves[4].append(va_s.reshape(db, t, cfg.n_kv, hd))
        leaves[5].append(ki_s.reshape(db, t, cfg.idx_dim))
        leaves[6].append(v_s.reshape(db, t, cfg.d_a))
    return (xp.reshape(nb, s, d), xs.reshape(db, t, d)) + tuple(jnp.stack(v) for v in leaves)


def kernel(x_prompt, x_sample, c_prompt, c_sample, cache_k, cache_v, cache_idx_k, page_table, mod_w, mod_b, norm_g, ffn_w_in, ffn_w_out, w_in, sgu_norm_g, sgu_w, sgu_b, q_norm_g, k_norm_g, w_branch_a, w_branch_b, w_out):
    return _forward(CFG, x_prompt, x_sample, c_prompt, c_sample, cache_k, cache_v, cache_idx_k, page_table,
                    mod_w, mod_b, norm_g, ffn_w_in, ffn_w_out, w_in, sgu_norm_g, sgu_w, sgu_b,
                    q_norm_g, k_norm_g, w_branch_a, w_branch_b, w_out)
```

```python
import collections
import functools

import jax
import jax.numpy as jnp
import numpy as np
from jax import lax
from jax.experimental import pallas as pl
from jax.experimental.pallas import tpu as pltpu

F32 = jnp.float32
BF16 = jnp.bfloat16
I32 = jnp.int32

LANES = 128
INT_MIN = -(2 ** 31)
NEG_BIG = -1e30
LOG2_E = 1.4426950408889634
VMEM_LIMIT = 56 * 1024 * 1024

Config = collections.namedtuple(
    "Config",
    "d_model seq depth dec_seq past_len page_size chunk d_a g_a n_heads head_dim n_kv "
    "n_idx_heads idx_dim topk_max rope_theta d_ff n_sub eps")

CFG = Config(d_model=1024, seq=8192, depth=2, dec_seq=4, past_len=16384, page_size=128, chunk=128,
             d_a=512, g_a=8, n_heads=8, head_dim=64, n_kv=4, n_idx_heads=8, idx_dim=64,
             topk_max=256, rope_theta=500000.0, d_ff=2816, n_sub=3, eps=1e-6)


def _dot(a, b):
    return lax.dot_general(a, b, (((1,), (0,)), ((), ())), preferred_element_type=F32)


def _dot_nt(a, b):
    return lax.dot_general(a, b, (((1,), (1,)), ((), ())), preferred_element_type=F32)


def _const_spec(shape):
    nd = len(shape)
    return pl.BlockSpec(shape, lambda *_: (0,) * nd, pipeline_mode=pl.Buffered(1))


def _params(n_grid):
    return pltpu.CompilerParams(dimension_semantics=("arbitrary",) * n_grid,
                                vmem_limit_bytes=VMEM_LIMIT)


def _mod_kernel(c_ref, w_ref, b_ref, o_ref):
    c = c_ref[...]
    s = (c * jax.nn.sigmoid(c)).astype(BF16)
    o_ref[0] = _dot(s, w_ref[0]) + b_ref[0]


def _modulation(c_all, mod_w, mod_b, tn=1152):
    depth, d, n = mod_w.shape
    r = c_all.shape[0]
    return pl.pallas_call(
        _mod_kernel,
        grid=(depth, n // tn),
        in_specs=[pl.BlockSpec((r, d), lambda l, j: (0, 0)),
                  pl.BlockSpec((1, d, tn), lambda l, j: (l, 0, j)),
                  pl.BlockSpec((1, 1, tn), lambda l, j: (l, 0, j))],
        out_specs=pl.BlockSpec((1, r, tn), lambda l, j: (l, 0, j)),
        out_shape=jax.ShapeDtypeStruct((depth, r, n), F32),
        compiler_params=_params(2),
        name="modulation",
    )(c_all, mod_w.astype(BF16), mod_b.reshape(depth, 1, n))


def _mod_row(mod_ref, idx, per_row):
    if per_row:
        return mod_ref[idx]
    return mod_ref[0, idx:idx + 1, :]


def _adaln(x, g, shift, scale, eps):
    y = x * lax.rsqrt(jnp.mean(x * x, axis=-1, keepdims=True) + eps)
    return (y * g) * (1.0 + scale) + shift


def _swiglu(h, wa_ref, wb_ref, w2_ref, n_chunks):
    d_ff = wa_ref.shape[1]
    cw = d_ff // n_chunks
    acc = None
    for c in range(n_chunks):
        a = _dot(h, wa_ref[:, c * cw:(c + 1) * cw])
        b = _dot(h, wb_ref[:, c * cw:(c + 1) * cw])
        t = ((a * jax.nn.sigmoid(a)) * b).astype(BF16)
        part = _dot(t, w2_ref[c * cw:(c + 1) * cw, :])
        acc = part if acc is None else acc + part
    return acc


def _ffn_kernel(*refs, merge, per_row, sub, n_chunks, eps):
    if merge:
        (x_ref, mod_ref, g_ref, ya_ref, gb_ref, ob_ref, pb_ref, wo_ref,
         wa_ref, wb_ref, w2_ref, o_ref) = refs
    else:
        x_ref, mod_ref, g_ref, wa_ref, wb_ref, w2_ref, o_ref = refs
    x = x_ref[...]
    if merge:
        y = ya_ref[...] + gb_ref[...] * _dot(ob_ref[...], pb_ref[...])
        x = x + _mod_row(mod_ref, 5, per_row) * _dot(y.astype(BF16), wo_ref[...])
    shift = _mod_row(mod_ref, 3 * sub, per_row)
    scale = _mod_row(mod_ref, 3 * sub + 1, per_row)
    gate = _mod_row(mod_ref, 3 * sub + 2, per_row)
    h = _adaln(x, g_ref[...], shift, scale, eps).astype(BF16)
    o_ref[...] = x + (0.5 * gate) * _swiglu(h, wa_ref, wb_ref, w2_ref, n_chunks)


def _ffn(x, mod, g, wa, wb, w2, cfg, *, sub, tm, rows_per_mod, merge_args=None):
    n, d = x.shape
    per_row = rows_per_mod is None
    merge = merge_args is not None
    row = lambda i: (i, 0)
    if per_row:
        mod_spec = pl.BlockSpec((mod.shape[0], tm, d), lambda i: (0, i, 0))
    else:
        tiles_per_mod = rows_per_mod // tm
        mod_spec = pl.BlockSpec((1, mod.shape[1], d), lambda i: (i // tiles_per_mod, 0, 0))
    in_specs = [pl.BlockSpec((tm, d), row), mod_spec, _const_spec((1, d))]
    args = [x, mod, g.reshape(1, d)]
    if merge:
        ya, gb, ob, pb, wo = merge_args
        in_specs += [pl.BlockSpec((tm, d), row), pl.BlockSpec((tm, d), row),
                     pl.BlockSpec((tm, ob.shape[1]), row), _const_spec(pb.shape), _const_spec(wo.shape)]
        args += [ya, gb, ob, pb, wo]
    in_specs += [_const_spec(wa.shape), _const_spec(wb.shape), _const_spec(w2.shape)]
    args += [wa, wb, w2]
    kern = functools.partial(_ffn_kernel, merge=merge, per_row=per_row, sub=sub, n_chunks=2, eps=cfg.eps)
    return pl.pallas_call(
        kern,
        grid=(n // tm,),
        in_specs=in_specs,
        out_specs=pl.BlockSpec((tm, d), row),
        out_shape=jax.ShapeDtypeStruct((n, d), F32),
        compiler_params=_params(1),
        name="merge_ffn" if merge else "ffn",
    )(*args)


def _gelu_tanh(x):
    return 0.5 * x * (1.0 + jnp.tanh(np.sqrt(2.0 / np.pi).astype(np.float32) * (x + 0.044715 * (x * x * x))))


def _head_mean(sq, bd_ref, head_dim):
    hi = sq.astype(BF16)
    lo = (sq - hi.astype(F32)).astype(BF16)
    return (_dot(hi, bd_ref[...]) + _dot(lo, bd_ref[...])) * (1.0 / head_dim)


def _rope(x, cos, sin_up, sin_dn, shift):
    parts = []
    for j in range(x.shape[1] // LANES):
        xs = x[:, j * LANES:(j + 1) * LANES]
        parts.append(xs * cos + pltpu.roll(xs, shift, 1) * sin_up + pltpu.roll(xs, LANES - shift, 1) * sin_dn)
    return parts[0] if len(parts) == 1 else jnp.concatenate(parts, axis=1)


def _proj_kernel(*refs, per_row, sample, cfg):
    (x_ref, mod_ref, g_ref, w1_ref, w2_ref, w3_ref, sg_ref, wm_ref, sb_ref, qg_ref, kg_ref,
     bdq_ref, bdk_ref, cos_ref, sup_ref, sdn_ref, pa_ref) = refs[:17]
    outs = refs[17:]
    eps, hd, d_a = cfg.eps, cfg.head_dim, cfg.d_a
    nq, nkv, nqi = cfg.n_heads * hd, cfg.n_kv * hd, cfg.n_idx_heads * cfg.idx_dim
    tm = x_ref.shape[0]

    x = x_ref[...]
    h = _adaln(x, g_ref[...], _mod_row(mod_ref, 3, per_row), _mod_row(mod_ref, 4, per_row), eps).astype(BF16)
    z1 = _dot(h, w1_ref[...])
    z2 = _dot(h, w2_ref[...])
    z3 = _dot(h, w3_ref[...])
    o = 0
    zu = z1[:, o:o + d_a]; o += d_a
    zv = z1[:, o:o + d_a]; o += d_a
    zq = z1[:, o:o + nq]; o += nq
    zk = z1[:, o:o + nkv]; o += nkv
    zva = z1[:, o:o + nkv]; o += nkv
    zqi = z1[:, o:o + nqi]

    u = _gelu_tanh(zu)
    gv = _gelu_tanh(zv)
    v = gv * lax.rsqrt(jnp.mean(gv * gv, axis=-1, keepdims=True) + eps) * sg_ref[...]
    vb = v.astype(BF16)
    ck = wm_ref.shape[1]
    rr = lax.broadcasted_iota(I32, (ck, ck), 0)
    cc = lax.broadcasted_iota(I32, (ck, ck), 1)
    lane_group = lax.broadcasted_iota(I32, (ck, d_a), 1) // (d_a // cfg.g_a)
    wms = [jnp.where(rr >= cc, wm_ref[g], 0.0).astype(BF16) for g in range(cfg.g_a)]
    mixed_chunks = []
    for c in range(tm // ck):
        vc = vb[c * ck:(c + 1) * ck, :]
        mixed = sb_ref[...]
        for g in range(cfg.g_a):
            mixed = mixed + jnp.where(lane_group == g, _dot(wms[g], vc), 0.0)
        mixed_chunks.append(mixed)
    mixed = mixed_chunks[0] if len(mixed_chunks) == 1 else jnp.concatenate(mixed_chunks, axis=0)
    o_a = (u * mixed).astype(BF16)
    ga = jax.nn.sigmoid(z3[:, :cfg.d_model])
    gb = jax.nn.sigmoid(z3[:, cfg.d_model:])
    ya = ga * _dot(o_a, pa_ref[...])

    cos, sup, sdn = cos_ref[...], sup_ref[...], sdn_ref[...]
    half = hd // 8
    q = zq * lax.rsqrt(_head_mean(zq * zq, bdq_ref, hd) + eps) * qg_ref[...]
    q = _rope(q, cos, sup, sdn, half)
    k = zk * lax.rsqrt(_head_mean(zk * zk, bdk_ref, hd) + eps) * kg_ref[...]
    k = _rope(k, cos, sup, sdn, half)
    qi = _rope(zqi, cos, sup, sdn, half)
    ki_wide = _rope(z2, cos, sup, sdn, half)
    ki = ki_wide[:, :cfg.idx_dim]
    wi_scale = cfg.n_idx_heads ** -0.5

    if sample:
        (k_ref, va_ref, ki_ref, v_ref, q_ref, qi_ref, wi_ref, ya_ref, gb_ref) = outs
        k_ref[...] = k
        va_ref[...] = zva
        ki_ref[...] = ki
        v_ref[...] = v
        q_ref[...] = q
        qi_ref[...] = qi
        wi_ref[...] = z2[:, cfg.idx_dim:cfg.idx_dim + cfg.n_idx_heads] * wi_scale
    else:
        (kt_ref, vat_ref, kit_ref, qt_ref, qit_ref, kh_ref, vt_ref, kib_ref, wi_ref, ya_ref, gb_ref) = outs
        kt_ref[0] = k.T
        vat = zva.T
        vat_ref[0] = vat
        kit_ref[0] = ki_wide.T[:cfg.idx_dim, :]
        qt_ref[0] = (q * (LOG2_E * hd ** -0.5)).T.astype(BF16)
        qit_ref[0] = qi.T.astype(BF16)
        vt_ref[0] = vat.astype(BF16)
        for hh in range(cfg.n_kv):
            kh_ref[0, hh] = k[:, hh * hd:(hh + 1) * hd].astype(BF16)
        kib_ref[0] = ki.astype(BF16)
        wi_ref[0] = z2.T[cfg.idx_dim:cfg.idx_dim + cfg.n_idx_heads, :] * wi_scale
    ya_ref[...] = ya
    gb_ref[...] = gb


def _rope_tables(pos, cfg):
    hd = cfg.head_dim
    rd = hd // 4
    half = rd // 2
    freqs = cfg.rope_theta ** (-jnp.arange(half, dtype=F32) * 2.0 / rd)
    ang = pos.astype(F32)[:, None] * freqs[None, :]
    cos, sin = jnp.cos(ang), jnp.sin(ang)
    n = pos.shape[0]
    one = jnp.ones((n, hd - rd), F32)
    zero = jnp.zeros((n, hd - rd), F32)
    zh = jnp.zeros((n, half), F32)
    cos_h = jnp.concatenate([cos, cos, one], axis=1)
    sup_h = jnp.concatenate([zh, sin, zero], axis=1)
    sdn_h = jnp.concatenate([-sin, zh, zero], axis=1)
    rep = LANES // hd
    return tuple(jnp.tile(t, (1, rep)) for t in (cos_h, sup_h, sdn_h))


def _proj_weights(lw, cfg):
    d_a, hd = cfg.d_a, cfg.head_dim
    nq, nkv, nqi = cfg.n_heads * hd, cfg.n_kv * hd, cfg.n_idx_heads * cfg.idx_dim
    n1 = 2 * d_a + nq + 2 * nkv + nqi
    n2 = cfg.idx_dim + cfg.n_idx_heads
    w_in = lw["w_in"]
    w1 = w_in[:, :n1].astype(BF16)
    w2 = jnp.pad(w_in[:, n1:n1 + n2], ((0, 0), (0, LANES - n2))).astype(BF16)
    w3 = w_in[:, n1 + n2:].astype(BF16)
    gd = d_a // cfg.g_a

    def block_diag(width):
        hidx = jnp.arange(width) // hd
        return (hidx[:, None] == hidx[None, :]).astype(BF16)

    return dict(
        w1=w1, w2=w2, w3=w3,
        sg=lw["sgu_norm_g"].reshape(1, d_a),
        qg=jnp.tile(lw["q_norm_g"], cfg.n_heads).reshape(1, nq),
        kg=jnp.tile(lw["k_norm_g"], cfg.n_kv).reshape(1, nkv),
        bdq=block_diag(nq), bdk=block_diag(nkv),
        pa=lw["w_branch_a"].astype(BF16),
        wm_p=lw["sgu_w"],
        sb_p=jnp.repeat(lw["sgu_b"].T, gd, axis=1),
    )


def _sample_gate_weights(lw, cfg, n_seq):
    t = cfg.dec_seq
    gd = cfg.d_a // cfg.g_a
    eye = jnp.eye(n_seq, dtype=F32)
    wm = jax.vmap(lambda w: jnp.kron(eye, w))(lw["sgu_w"][:, :t, :t])
    sb = jnp.tile(jnp.repeat(lw["sgu_b"][:, :t].T, gd, axis=1), (n_seq, 1))
    return wm, sb


def _proj(x, mod, g, pw, wm, sb, tabs, cfg, *, tm, rows_per_mod, sample, n_batch=None):
    n, d = x.shape
    per_row = rows_per_mod is None
    row = lambda i: (i, 0)
    if per_row:
        mod_spec = pl.BlockSpec((mod.shape[0], tm, d), lambda i: (0, i, 0))
        tab_spec = pl.BlockSpec((tm, LANES), row)
    else:
        tiles_per_mod = rows_per_mod // tm
        mod_spec = pl.BlockSpec((1, mod.shape[1], d), lambda i: (i // tiles_per_mod, 0, 0))
        tab_spec = pl.BlockSpec((tm, LANES), lambda i: (i % tiles_per_mod, 0))
    consts = [g.reshape(1, d), pw["w1"], pw["w2"], pw["w3"], pw["sg"], wm, sb, pw["qg"], pw["kg"],
              pw["bdq"], pw["bdk"]]
    in_specs = ([pl.BlockSpec((tm, d), row), mod_spec] + [_const_spec(a.shape) for a in consts]
                + [tab_spec] * 3 + [_const_spec(pw["pa"].shape)])
    args = [x, mod] + consts + list(tabs) + [pw["pa"]]
    hd = cfg.head_dim
    nq, nkv, nqi = cfg.n_heads * hd, cfg.n_kv * hd, cfg.n_idx_heads * cfg.idx_dim

    def rows(width, dtype=F32):
        return pl.BlockSpec((tm, width), row), jax.ShapeDtypeStruct((n, width), dtype)

    if sample:
        outs = [rows(nkv), rows(nkv), rows(cfg.idx_dim), rows(cfg.d_a), rows(nq), rows(nqi),
                rows(cfg.n_idx_heads)]
    else:
        s = rows_per_mod
        tps = s // tm

        def transposed(width, dtype=BF16):
            return (pl.BlockSpec((1, width, tm), lambda i: (i // tps, 0, i % tps)),
                    jax.ShapeDtypeStruct((n_batch, width, s), dtype))

        outs = [transposed(nkv, F32), transposed(nkv, F32), transposed(cfg.idx_dim, F32),
                transposed(nq), transposed(nqi),
                 (pl.BlockSpec((1, cfg.n_kv, tm, hd), lambda i: (i // tps, 0, i % tps, 0)),
                  jax.ShapeDtypeStruct((n_batch, cfg.n_kv, s, hd), BF16)),
                 transposed(nkv),
                 (pl.BlockSpec((1, tm, cfg.idx_dim), lambda i: (i // tps, i % tps, 0)),
                  jax.ShapeDtypeStruct((n_batch, s, cfg.idx_dim), BF16)),
                 transposed(cfg.n_idx_heads, F32)]
    outs += [rows(d), rows(d)]
    kern = functools.partial(_proj_kernel, per_row=per_row, sample=sample, cfg=cfg)
    return pl.pallas_call(
        kern,
        grid=(n // tm,),
        in_specs=in_specs,
        out_specs=[o[0] for o in outs],
        out_shape=[o[1] for o in outs],
        compiler_params=_params(1),
        name="proj_sample" if sample else "proj_prompt",
    )(*args)


def _score_key(x):
    b = lax.bitcast_convert_type(x + 0.0, I32)
    return b ^ ((b >> 31) & 0x7FFFFFFF)


SUBLANES = 8
BF16_ROWS = 16


class _KeyBlock:
    def __init__(self, ref, key_axis, tk, tiles_per_trip=1):
        self.ref, self.key_axis, self.tk, self.tiles_per_trip = ref, key_axis, tk, tiles_per_trip
        self.n_vec = ref.shape[1 - key_axis]
        if key_axis == 1:
            self.unit = (self.n_vec, LANES)
            self.pos = lax.broadcasted_iota(I32, self.unit, 1)
        else:
            self.unit = (SUBLANES, self.n_vec)
            self.pos = lax.broadcasted_iota(I32, (tk, self.n_vec), 0).reshape(
                tk // SUBLANES, SUBLANES, self.n_vec)

    def bcast(self, a):
        return jnp.broadcast_to(a, self.unit)

    def tile(self, off):
        if self.key_axis == 1:
            x = self.ref[:, pl.ds(off, self.tk)]
            return [(x[:, c * LANES:(c + 1) * LANES], self.pos + (off + c * LANES))
                    for c in range(self.tk // LANES)]
        x = self.ref[pl.ds(off, self.tk), :]
        return [(x.reshape(self.tk // SUBLANES, SUBLANES, self.n_vec), self.pos + off)]

    def count(self, n_tiles, pred):
        group = self.tiles_per_trip

        def body(j, acc):
            for u in range(group):
                for x, pos in self.tile(pl.multiple_of((j * group + u) * self.tk, self.tk)):
                    hits = jnp.where(pred(x, pos), 1, 0)
                    acc = acc + (hits if self.key_axis == 1 else jnp.sum(hits, axis=0))
            return acc

        n_trips = n_tiles if group == 1 else (n_tiles + group - 1) // group
        acc = lax.fori_loop(0, n_trips, body, jnp.zeros(self.unit, I32))
        return jnp.sum(acc.astype(F32), axis=self.key_axis, keepdims=True).astype(I32)


def _select_threshold(kb, n_tiles, k_eff, live=None):
    def count_ge(cand):
        cb = kb.bcast(cand)
        return kb.count(n_tiles, lambda x, _: x >= cb)

    zero = jnp.zeros_like(k_eff)
    c0 = count_ge(zero)
    c1 = count_ge(zero + 1)
    t0 = jnp.where(c0 >= k_eff, 0, INT_MIN).astype(I32)
    zero_final = jnp.where(c0 >= k_eff, jnp.where(c1 < k_eff, 1, 0), 0).astype(I32)
    tied0 = jnp.where(c0 > k_eff, zero_final, 0)
    open0 = 1 - zero_final
    if live is not None:
        open0 = jnp.where(live, open0, 0)
        tied0 = jnp.where(live, tied0, 0)

    def any_set(flags):
        return jnp.max(flags.astype(F32)) > 0.5

    def cond(state):
        it, _, opn = state
        return jnp.logical_and(it < 31, any_set(opn))

    def body(state):
        it, t, opn = state
        cand = t + lax.shift_left(jnp.int32(1), 30 - it)
        c = count_ge(cand)
        t = jnp.where(opn > 0, jnp.where(c >= k_eff, cand, t), t)
        opn = jnp.where(c == k_eff, 0, opn)
        return it + 1, t, opn

    _, t, opn = lax.while_loop(cond, body, (jnp.int32(0), t0, open0))
    return t, any_set(opn + tied0)


def _tie_need(kb, n_tiles, k_eff, t):
    tb = kb.bcast(t)
    return (k_eff - kb.count(n_tiles, lambda x, _: x > tb)).astype(F32)


def _selection_bias(x, t, tie=None):
    if tie is None:
        return jnp.where(x >= t, 0.0, NEG_BIG)
    prefix, need = tie
    return jnp.where(x > t, 0.0, jnp.where(x < t, NEG_BIG, jnp.where(prefix <= need, 0.0, NEG_BIG)))


def _tie_indicator(x, t):
    return jnp.where(x == t, 1.0, 0.0).astype(BF16)


def _softmax_step(s, pv, m_ref, l_ref, acc_ref, idx, key_axis, exp=jnp.exp):
    m_old = m_ref[idx]
    m_new = jnp.maximum(m_old, jnp.max(s, axis=key_axis, keepdims=True))
    p = exp(s - m_new)
    alpha = exp(m_old - m_new)
    l_ref[idx] = alpha * l_ref[idx] + jnp.sum(p, axis=key_axis, keepdims=True)
    acc_ref[idx] = alpha * acc_ref[idx] + pv(p.astype(BF16))
    m_ref[idx] = m_new


def _prompt_attn_kernel(qit_ref, wt_ref, ki_ref, qt_ref, k_ref, vt_ref, o_ref,
                        sc_ref, bias_ref, s_ref, m_ref, acc_ref, *, qb, topk, cfg):
    i = pl.program_id(1)
    tk = qb
    n_tiles = i + 1
    q_pos = i * qb + lax.broadcasted_iota(I32, (1, qb), 1)
    key_pos = lax.broadcasted_iota(I32, (tk, qb), 0)
    wt = wt_ref[0]
    rep = cfg.n_heads // cfg.n_kv
    hd, di = cfg.head_dim, cfg.idx_dim

    def score_tile(j, carry):
        off = pl.multiple_of(j * tk, tk)
        kib = ki_ref[0, pl.ds(off, tk), :]
        acc = jnp.zeros((tk, qb), F32)
        for h in range(cfg.n_idx_heads):
            acc = acc + wt[h:h + 1, :] * jnp.maximum(_dot(kib, qit_ref[0, h * di:(h + 1) * di, :]), 0.0)
        sc_ref[pl.ds(off, tk), :] = jnp.where(key_pos + off <= q_pos, _score_key(acc), INT_MIN)
        return carry

    lax.fori_loop(0, n_tiles, score_tile, 0)

    pair = 2 if (sc_ref.shape[0] // tk) % 2 == 0 else 1
    if pair == 2:
        @pl.when(n_tiles % 2 == 1)
        def _():
            sc_ref[pl.ds(pl.multiple_of(n_tiles * tk, tk), tk), :] = jnp.full((tk, qb), INT_MIN, I32)

    kb = _KeyBlock(sc_ref, 0, tk, tiles_per_trip=pair)
    k_eff = jnp.minimum(q_pos + 1, topk)
    t, has_ties = _select_threshold(kb, n_tiles, k_eff)
    need = lax.cond(has_ties, lambda: _tie_need(kb, n_tiles, k_eff, t), lambda: jnp.zeros((1, qb), F32))

    m_ref[...] = jnp.full(m_ref.shape, NEG_BIG, F32)
    acc_ref[...] = jnp.zeros(acc_ref.shape, F32)
    ones_rows = jnp.where(lax.broadcasted_iota(I32, (BF16_ROWS, tk), 0) == 0, 1.0, 0.0).astype(BF16)
    prefix_mat = jnp.where(key_pos[:, :tk] >= lax.broadcasted_iota(I32, (tk, tk), 1), 1.0, 0.0).astype(BF16)

    def attend_tile(j, tied_before):
        off = pl.multiple_of(j * tk, tk)
        x = sc_ref[pl.ds(off, tk), :]

        def tie_bias(tied_before):
            prefix = _dot(prefix_mat, _tie_indicator(x, t)) + tied_before
            bias_ref[...] = _selection_bias(x, t, (prefix, need))
            return prefix[tk - 1:tk, :]

        def plain_bias(tied_before):
            bias_ref[...] = _selection_bias(x, t)
            return tied_before

        tied_before = lax.cond(has_ties, tie_bias, plain_bias, tied_before)
        bias = jnp.concatenate([bias_ref[...]] * rep, axis=1)
        for g in range(cfg.n_kv):
            qt = jnp.concatenate([qt_ref[0, h * hd:(h + 1) * hd, :] for h in range(g * rep, (g + 1) * rep)],
                                 axis=1)
            s_ref[g] = _dot(k_ref[0, g, pl.ds(off, tk), :], qt) + bias
        for g in range(cfg.n_kv):
            vt = jnp.concatenate([vt_ref[0, g * hd:(g + 1) * hd, pl.ds(off, tk)], ones_rows], axis=0)
            s = s_ref[g]
            m_old = m_ref[g]
            m_new = jnp.maximum(m_old, jnp.max(s, axis=0, keepdims=True))
            p = jnp.exp2(s - m_new).astype(BF16)
            acc_ref[g] = jnp.exp2(m_old - m_new) * acc_ref[g] + _dot(vt, p)
            m_ref[g] = m_new
        return tied_before

    lax.fori_loop(0, n_tiles, attend_tile, jnp.zeros((1, qb), F32))

    heads = []
    for h in range(cfg.n_heads):
        g, r = divmod(h, rep)
        a = acc_ref[g, :, r * qb:(r + 1) * qb]
        heads.append(a[:hd] / a[hd:hd + 1])
    o_ref[0] = jnp.concatenate(heads, axis=0).T.astype(BF16)


def _prompt_attention(qit, wt, kib, qt, kh, vt, cfg, *, qb=256):
    nb, _, s, hd = kh.shape
    qb = min(qb, s)
    topk = min(cfg.topk_max, s // 4)
    rep = cfg.n_heads // cfg.n_kv
    kern = functools.partial(_prompt_attn_kernel, qb=qb, topk=topk, cfg=cfg)
    q_block = lambda b, i: (b, 0, i)

    def per_batch(shape):
        nd = len(shape)
        return pl.BlockSpec((1,) + shape, lambda b, i: (b,) + (0,) * nd, pipeline_mode=pl.Buffered(1))

    return pl.pallas_call(
        kern,
        grid=(nb, s // qb),
        in_specs=[pl.BlockSpec((1, cfg.n_idx_heads * cfg.idx_dim, qb), q_block),
                  pl.BlockSpec((1, cfg.n_idx_heads, qb), q_block),
                  per_batch((s, cfg.idx_dim)),
                  pl.BlockSpec((1, cfg.n_heads * hd, qb), q_block),
                  per_batch((cfg.n_kv, s, hd)),
                  per_batch((cfg.n_kv * hd, s))],
        out_specs=pl.BlockSpec((1, qb, cfg.n_heads * hd), lambda b, i: (b, i, 0)),
        out_shape=jax.ShapeDtypeStruct((nb, s, cfg.n_heads * hd), BF16),
        scratch_shapes=[pltpu.VMEM((s, qb), I32),
                        pltpu.VMEM((qb, qb), F32),
                        pltpu.VMEM((cfg.n_kv, qb, rep * qb), F32),
                        pltpu.VMEM((cfg.n_kv, 1, rep * qb), F32),
                        pltpu.VMEM((cfg.n_kv, hd + BF16_ROWS, rep * qb), F32)],
        compiler_params=_params(2),
        name="prompt_attention",
    )(qit, wt, kib, qt, kh, vt)


TOK_PAD = 8
PAGES_PER_STEP = 32
SEQS_PER_SEARCH = 8


def _samp_index_kernel(pt_ref, qi_ref, w_ref, kit_ref, *refs, n_pages_step, n_chunks, tk, topk, cfg):
    page_refs = refs[:n_pages_step]
    bias_ref, sc_ref = refs[n_pages_step:]
    step = pl.program_id(1)
    seq = step // n_chunks
    c = step % n_chunks
    nh = cfg.n_idx_heads
    qi = qi_ref[0]
    w = w_ref[0]
    width = n_pages_step * cfg.page_size
    rows, total = sc_ref.shape
    seq_rows = pl.ds(pl.multiple_of(seq * TOK_PAD, TOK_PAD), TOK_PAD)

    def token_scores(keys_t):
        y = w * jnp.maximum(_dot(qi, keys_t), 0.0)
        return jnp.sum(y.reshape(nh, TOK_PAD, y.shape[1]), axis=0)

    kc = jnp.concatenate([r[...] for r in page_refs], axis=1).astype(BF16)
    sc_ref[seq_rows, pl.ds(pl.multiple_of(c * width, width), width)] = _score_key(token_scores(kc))

    @pl.when(c == n_chunks - 1)
    def _():
        tail = kit_ref.shape[2]
        tok = lax.broadcasted_iota(I32, (TOK_PAD, tail), 0)
        new = lax.broadcasted_iota(I32, (TOK_PAD, tail), 1)
        valid = new <= jnp.minimum(tok, cfg.dec_seq - 1)
        sc_ref[seq_rows, total - tail:] = jnp.where(valid, _score_key(token_scores(kit_ref[0])), INT_MIN)

    @pl.when(step == pl.num_programs(1) - 1)
    def _():
        tok1 = lax.broadcasted_iota(I32, (rows, 1), 0) % TOK_PAD
        k_eff = jnp.minimum(cfg.past_len + jnp.minimum(tok1, cfg.dec_seq - 1) + 1, topk)
        kb = _KeyBlock(sc_ref, 1, tk)
        t, _ = _select_threshold(kb, total // tk, k_eff, live=tok1 < cfg.dec_seq)
        need = _tie_need(kb, total // tk, k_eff, t)
        prefix_mat = jnp.where(lax.broadcasted_iota(I32, (LANES, LANES), 0)
                               <= lax.broadcasted_iota(I32, (LANES, LANES), 1), 1.0, 0.0).astype(BF16)

        tied_before = jnp.zeros((rows, 1), F32)
        for j in range(total // LANES):
            x = sc_ref[:, j * LANES:(j + 1) * LANES]
            tied = _tie_indicator(x, t)
            prefix = _dot(tied, prefix_mat) + tied_before
            bias_ref[:, j * LANES:(j + 1) * LANES] = _selection_bias(x, t, (prefix, need))
            tied_before = tied_before + jnp.sum(tied.astype(F32), axis=1, keepdims=True)


def _samp_attn_kernel(pt_ref, q_ref, bias_ref, biast_ref, kt_ref, vt_ref, *refs, n_pages_step, cfg):
    k_pages = refs[:n_pages_step]
    v_pages = refs[n_pages_step:2 * n_pages_step]
    o_ref, m_ref, l_ref, acc_ref = refs[2 * n_pages_step:]
    c = pl.program_id(1)
    n_chunks = pl.num_programs(1)
    q = q_ref[0]
    hd = cfg.head_dim
    rep = cfg.n_heads // cfg.n_kv

    @pl.when(c == 0)
    def _():
        m_ref[...] = jnp.full(m_ref.shape, NEG_BIG, F32)
        l_ref[...] = jnp.zeros(l_ref.shape, F32)
        acc_ref[...] = jnp.zeros(acc_ref.shape, F32)

    def step(kt, vt, bias):
        s = _dot(q, kt) + jnp.concatenate([bias] * cfg.n_heads, axis=0)
        _softmax_step(s, lambda p: _dot_nt(p, vt), m_ref, l_ref, acc_ref, 0, 1)

    kc = jnp.concatenate([r[...] for r in k_pages], axis=1).astype(BF16)
    vc = jnp.concatenate([r[...] for r in v_pages], axis=1).astype(BF16)
    step(kc, vc, bias_ref[0])

    @pl.when(c == n_chunks - 1)
    def _():
        step(kt_ref[0], vt_ref[0], biast_ref[0])
        o = acc_ref[0] / l_ref[0]
        for h in range(cfg.n_heads):
            g = h // rep
            o_ref[0, h] = o[h * TOK_PAD:(h + 1) * TOK_PAD, g * hd:(g + 1) * hd]


def _sample_attention(l, q, qi, wi, k_new, v_new, ki_new, cache_k, cache_v, cache_idx_k, page_table, cfg):
    db, n_pages = page_table.shape
    t, hd, ps = cfg.dec_seq, cfg.head_dim, cfg.page_size
    nh, nkv, nih, di = cfg.n_heads, cfg.n_kv, cfg.n_idx_heads, cfg.idx_dim
    rep = nh // nkv
    pstep = min(PAGES_PER_STEP, n_pages)
    n_chunks = n_pages // pstep
    width = pstep * ps
    tail = LANES
    total = cfg.past_len + tail
    topk = min(cfg.topk_max, (cfg.past_len + t) // 4)
    n_lane_groups = total // LANES
    tk = LANES * max(d for d in range(1, 49) if n_lane_groups % d == 0)

    def head_major(a, nheads, dim):
        a = a.reshape(db, t, nheads, dim).transpose(0, 2, 1, 3)
        return jnp.pad(a, ((0, 0), (0, 0), (0, TOK_PAD - t), (0, 0)))

    qi_rows = head_major(qi, nih, di).reshape(db, nih * TOK_PAD, di).astype(BF16)
    w_rows = head_major(wi, nih, 1).reshape(db, nih * TOK_PAD, 1)
    group = (jnp.arange(nh) // rep)[:, None] == jnp.arange(nkv)[None, :]
    q_rows = head_major(q * (hd ** -0.5), nh, hd)[:, :, :, None, :] * group[None, :, None, :, None].astype(F32)
    q_rows = q_rows.reshape(db, nh * TOK_PAD, nkv * hd).astype(BF16)

    def tail_pad(a):
        a = a.reshape(db, t, a.shape[-1]).transpose(0, 2, 1)
        return jnp.pad(a, ((0, 0), (0, 0), (0, tail - t))).astype(BF16)

    ki_tail, k_tail, v_tail = tail_pad(ki_new), tail_pad(k_new), tail_pad(v_new)
    ck = cache_k.transpose(0, 1, 3, 4, 2).reshape(cache_k.shape[:2] + (nkv * hd, ps))
    cv = cache_v.transpose(0, 1, 3, 4, 2).reshape(cache_v.shape[:2] + (nkv * hd, ps))
    cik = cache_idx_k.transpose(0, 1, 3, 2)

    def page_spec(j, w):
        return pl.BlockSpec((None, None, w, ps), lambda b, c, pt: (l, pt[b, c * pstep + j], 0, 0))

    per_seq3 = lambda b, c, pt: (b, 0, 0)
    grp = max(g for g in range(1, SEQS_PER_SEARCH + 1) if db % g == 0)
    grp_seq3 = lambda g, s, pt: (g * grp + s // n_chunks, 0, 0)

    def grp_page_spec(j, w):
        return pl.BlockSpec((None, None, w, ps), lambda g, s, pt: (
            l, pt[g * grp + s // n_chunks, (s % n_chunks) * pstep + j], 0, 0))

    bias = pl.pallas_call(
        functools.partial(_samp_index_kernel, n_pages_step=pstep, n_chunks=n_chunks, tk=tk, topk=topk, cfg=cfg),
        grid_spec=pltpu.PrefetchScalarGridSpec(
            num_scalar_prefetch=1,
            grid=(db // grp, grp * n_chunks),
            in_specs=[pl.BlockSpec((1, nih * TOK_PAD, di), grp_seq3),
                      pl.BlockSpec((1, nih * TOK_PAD, 1), grp_seq3),
                      pl.BlockSpec((1, di, tail), grp_seq3)]
                     + [grp_page_spec(j, di) for j in range(pstep)],
            out_specs=pl.BlockSpec((grp * TOK_PAD, total), lambda g, s, pt: (g, 0)),
            scratch_shapes=[pltpu.VMEM((grp * TOK_PAD, total), I32)]),
        out_shape=jax.ShapeDtypeStruct((db * TOK_PAD, total), F32),
        compiler_params=_params(2),
        name="sample_index",
    )(page_table, qi_rows, w_rows, ki_tail, *([cik] * pstep)).reshape(db, TOK_PAD, total)

    out = pl.pallas_call(
        functools.partial(_samp_attn_kernel, n_pages_step=pstep, cfg=cfg),
        grid_spec=pltpu.PrefetchScalarGridSpec(
            num_scalar_prefetch=1,
            grid=(db, n_chunks),
            in_specs=[pl.BlockSpec((1, nh * TOK_PAD, nkv * hd), per_seq3),
                      pl.BlockSpec((1, TOK_PAD, width), lambda b, c, pt: (b, 0, c)),
                      pl.BlockSpec((1, TOK_PAD, tail), lambda b, c, pt: (b, 0, total // tail - 1)),
                      pl.BlockSpec((1, nkv * hd, tail), per_seq3),
                      pl.BlockSpec((1, nkv * hd, tail), per_seq3)]
                     + [page_spec(j, nkv * hd) for j in range(pstep)] * 2,
            out_specs=pl.BlockSpec((1, nh, TOK_PAD, hd), lambda b, c, pt: (b, 0, 0, 0)),
            scratch_shapes=[pltpu.VMEM((1, nh * TOK_PAD, 1), F32),
                            pltpu.VMEM((1, nh * TOK_PAD, 1), F32),
                            pltpu.VMEM((1, nh * TOK_PAD, nkv * hd), F32)]),
        out_shape=jax.ShapeDtypeStruct((db, nh, TOK_PAD, hd), F32),
        compiler_params=_params(2),
        name="sample_attention",
    )(page_table, q_rows, bias, bias, k_tail, v_tail, *([ck] * pstep), *([cv] * pstep))
    return out[:, :, :t, :].transpose(0, 2, 1, 3).reshape(db * t, nh * hd).astype(BF16)


def _forward(cfg, x_prompt, x_sample, c_prompt, c_sample, cache_k, cache_v, cache_idx_k, page_table,
             mod_w, mod_b, norm_g, ffn_w_in, ffn_w_out, w_in, sgu_norm_g, sgu_w, sgu_b,
             q_norm_g, k_norm_g, w_branch_a, w_branch_b, w_out):
    nb, s, d = x_prompt.shape
    db, t, _ = x_sample.shape
    depth = mod_w.shape[0]
    hd = cfg.head_dim
    n_mod = cfg.n_sub * 3

    n_seq = nb + db
    r_pad = -(-n_seq // 8) * 8
    c_all = jnp.pad(jnp.concatenate([c_prompt, c_sample], axis=0), ((0, r_pad - n_seq), (0, 0)))
    mod = _modulation(c_all, mod_w, mod_b).reshape(depth, r_pad, n_mod, d)
    mod_p = mod[:, :nb]
    mod_s = jnp.repeat(mod[:, nb:n_seq], t, axis=1).transpose(0, 2, 1, 3)

    tabs_p = _rope_tables(jnp.arange(s), cfg)
    tabs_s = _rope_tables(jnp.tile(cfg.past_len + jnp.arange(t), db), cfg)

    xp = x_prompt.reshape(nb * s, d)
    xs = x_sample.reshape(db * t, d)
    tm_ffn = min(512, s)
    tm_proj = min(256, s)
    d_ff = ffn_w_out.shape[2]
    leaves = [[] for _ in range(7)]
    for l in range(depth):
        lw = dict(w_in=w_in[l], sgu_norm_g=sgu_norm_g[l], sgu_w=sgu_w[l], sgu_b=sgu_b[l],
                  q_norm_g=q_norm_g[l], k_norm_g=k_norm_g[l], w_branch_a=w_branch_a[l])
        pw = _proj_weights(lw, cfg)
        wm_s, sb_s = _sample_gate_weights(lw, cfg, db)
        ffn_w = [(ffn_w_in[l, j, :, :d_ff].astype(BF16), ffn_w_in[l, j, :, d_ff:].astype(BF16),
                  ffn_w_out[l, j].astype(BF16)) for j in range(2)]
        pb = w_branch_b[l].astype(BF16)
        wo = w_out[l].astype(BF16)

        xp = _ffn(xp, mod_p[l], norm_g[l, 0], *ffn_w[0], cfg, sub=0, tm=tm_ffn, rows_per_mod=s)
        (k_p, va_p, ki_p, qt, qit, kh, vt, kib, wt, ya_p, gb_p) = _proj(
            xp, mod_p[l], norm_g[l, 1], pw, pw["wm_p"], pw["sb_p"], tabs_p, cfg,
            tm=tm_proj, rows_per_mod=s, sample=False, n_batch=nb)
        ob_p = _prompt_attention(qit, wt, kib, qt, kh, vt, cfg)
        xp = _ffn(xp, mod_p[l], norm_g[l, 2], *ffn_w[1], cfg, sub=2, tm=tm_ffn, rows_per_mod=s,
                  merge_args=(ya_p, gb_p, ob_p.reshape(nb * s, -1), pb, wo))

        xs = _ffn(xs, mod_s[l], norm_g[l, 0], *ffn_w[0], cfg, sub=0, tm=db * t, rows_per_mod=None)
        (k_s, va_s, ki_s, v_s, q_s, qi_s, wi_s, ya_s, gb_s) = _proj(
            xs, mod_s[l], norm_g[l, 1], pw, wm_s, sb_s, tabs_s, cfg,
            tm=db * t, rows_per_mod=None, sample=True)
        ob_s = _sample_attention(l, q_s, qi_s, wi_s, k_s, va_s, ki_s, cache_k, cache_v, cache_idx_k,
                                 page_table, cfg)
        xs = _ffn(xs, mod_s[l], norm_g[l, 2], *ffn_w[1], cfg, sub=2, tm=db * t, rows_per_mod=None,
                  merge_args=(ya_s, gb_s, ob_s, pb, wo))

        leaves[0].append(k_p.reshape(nb, cfg.n_kv, hd, s).transpose(0, 3, 1, 2))
        leaves[1].append(va_p.reshape(nb, cfg.n_kv, hd, s).transpose(0, 3, 1, 2))
        leaves[2].append(ki_p.transpose(0, 2, 1))
        leaves[3].append(k_s.reshape(db, t, cfg.n_kv, hd))
        leaves[4].append(va_s.reshape(db, t, cfg.n_kv, hd))
        leaves[5].append(ki_s.reshape(db, t, cfg.idx_dim))
        leaves[6].append(v_s.reshape(db, t, cfg.d_a))
    return (xp.reshape(nb, s, d), xs.reshape(db, t, d)) + tuple(jnp.stack(v) for v in leaves)


def kernel(x_prompt, x_sample, c_prompt, c_sample, cache_k, cache_v, cache_idx_k, page_table, mod_w, mod_b, norm_g, ffn_w_in, ffn_w_out, w_in, sgu_norm_g, sgu_w, sgu_b, q_norm_g, k_norm_g, w_branch_a, w_branch_b, w_out):
    return _forward(CFG, x_prompt, x_sample, c_prompt, c_sample, cache_k, cache_v, cache_idx_k, page_table,
                    mod_w, mod_b, norm_g, ffn_w_in, ffn_w_out, w_in, sgu_norm_g, sgu_w, sgu_b,
                    q_norm_g, k_norm_g, w_branch_a, w_branch_b, w_out)
```

```python
import collections
import functools

import jax
import jax.numpy as jnp
import numpy as np
from jax import lax
from jax.experimental import pallas as pl
from jax.experimental.pallas import tpu as pltpu

F32 = jnp.float32
BF16 = jnp.bfloat16
I32 = jnp.int32

LANES = 128
INT_MIN = -(2 ** 31)
NEG_BIG = -1e30
LOG2_E = 1.4426950408889634
VMEM_LIMIT = 56 * 1024 * 1024

Config = collections.namedtuple(
    "Config",
    "d_model seq depth dec_seq past_len page_size chunk d_a g_a n_heads head_dim n_kv "
    "n_idx_heads idx_dim topk_max rope_theta d_ff n_sub eps")

CFG = Config(d_model=1024, seq=8192, depth=2, dec_seq=4, past_len=16384, page_size=128, chunk=128,
             d_a=512, g_a=8, n_heads=8, head_dim=64, n_kv=4, n_idx_heads=8, idx_dim=64,
             topk_max=256, rope_theta=500000.0, d_ff=2816, n_sub=3, eps=1e-6)


def _dot(a, b):
    return lax.dot_general(a, b, (((1,), (0,)), ((), ())), preferred_element_type=F32)


def _dot_nt(a, b):
    return lax.dot_general(a, b, (((1,), (1,)), ((), ())), preferred_element_type=F32)


def _const_spec(shape):
    nd = len(shape)
    return pl.BlockSpec(shape, lambda *_: (0,) * nd, pipeline_mode=pl.Buffered(1))


def _params(n_grid):
    return pltpu.CompilerParams(dimension_semantics=("arbitrary",) * n_grid,
                                vmem_limit_bytes=VMEM_LIMIT)


def _mod_kernel(c_ref, w_ref, b_ref, o_ref):
    c = c_ref[...]
    s = (c * jax.nn.sigmoid(c)).astype(BF16)
    o_ref[0] = _dot(s, w_ref[0]) + b_ref[0]


def _modulation(c_all, mod_w, mod_b, tn=1152):
    depth, d, n = mod_w.shape
    r = c_all.shape[0]
    return pl.pallas_call(
        _mod_kernel,
        grid=(depth, n // tn),
        in_specs=[pl.BlockSpec((r, d), lambda l, j: (0, 0)),
                  pl.BlockSpec((1, d, tn), lambda l, j: (l, 0, j)),
                  pl.BlockSpec((1, 1, tn), lambda l, j: (l, 0, j))],
        out_specs=pl.BlockSpec((1, r, tn), lambda l, j: (l, 0, j)),
        out_shape=jax.ShapeDtypeStruct((depth, r, n), F32),
        compiler_params=_params(2),
        name="modulation",
    )(c_all, mod_w.astype(BF16), mod_b.reshape(depth, 1, n))


def _mod_row(mod_ref, idx, per_row):
    if per_row:
        return mod_ref[idx]
    return mod_ref[0, idx:idx + 1, :]


def _adaln(x, g, shift, scale, eps):
    y = x * lax.rsqrt(jnp.mean(x * x, axis=-1, keepdims=True) + eps)
    return (y * g) * (1.0 + scale) + shift


def _swiglu(h, wa_ref, wb_ref, w2_ref, n_chunks):
    d_ff = wa_ref.shape[1]
    cw = d_ff // n_chunks
    acc = None
    for c in range(n_chunks):
        a = _dot(h, wa_ref[:, c * cw:(c + 1) * cw])
        b = _dot(h, wb_ref[:, c * cw:(c + 1) * cw])
        t = ((a * jax.nn.sigmoid(a)) * b).astype(BF16)
        part = _dot(t, w2_ref[c * cw:(c + 1) * cw, :])
        acc = part if acc is None else acc + part
    return acc


def _ffn_kernel(*refs, merge, per_row, sub, n_chunks, eps):
    if merge:
        (x_ref, mod_ref, g_ref, ya_ref, gb_ref, ob_ref, pb_ref, wo_ref,
         wa_ref, wb_ref, w2_ref, o_ref) = refs
    else:
        x_ref, mod_ref, g_ref, wa_ref, wb_ref, w2_ref, o_ref = refs
    x = x_ref[...]
    if merge:
        y = ya_ref[...] + gb_ref[...] * _dot(ob_ref[...], pb_ref[...])
        x = x + _mod_row(mod_ref, 5, per_row) * _dot(y.astype(BF16), wo_ref[...])
    shift = _mod_row(mod_ref, 3 * sub, per_row)
    scale = _mod_row(mod_ref, 3 * sub + 1, per_row)
    gate = _mod_row(mod_ref, 3 * sub + 2, per_row)
    h = _adaln(x, g_ref[...], shift, scale, eps).astype(BF16)
    o_ref[...] = x + (0.5 * gate) * _swiglu(h, wa_ref, wb_ref, w2_ref, n_chunks)


def _ffn(x, mod, g, wa, wb, w2, cfg, *, sub, tm, rows_per_mod, merge_args=None):
    n, d = x.shape
    per_row = rows_per_mod is None
    merge = merge_args is not None
    row = lambda i: (i, 0)
    if per_row:
        mod_spec = pl.BlockSpec((mod.shape[0], tm, d), lambda i: (0, i, 0))
    else:
        tiles_per_mod = rows_per_mod // tm
        mod_spec = pl.BlockSpec((1, mod.shape[1], d), lambda i: (i // tiles_per_mod, 0, 0))
    in_specs = [pl.BlockSpec((tm, d), row), mod_spec, _const_spec((1, d))]
    args = [x, mod, g.reshape(1, d)]
    if merge:
        ya, gb, ob, pb, wo = merge_args
        in_specs += [pl.BlockSpec((tm, d), row), pl.BlockSpec((tm, d), row),
                     pl.BlockSpec((tm, ob.shape[1]), row), _const_spec(pb.shape), _const_spec(wo.shape)]
        args += [ya, gb, ob, pb, wo]
    in_specs += [_const_spec(wa.shape), _const_spec(wb.shape), _const_spec(w2.shape)]
    args += [wa, wb, w2]
    kern = functools.partial(_ffn_kernel, merge=merge, per_row=per_row, sub=sub, n_chunks=2, eps=cfg.eps)
    return pl.pallas_call(
        kern,
        grid=(n // tm,),
        in_specs=in_specs,
        out_specs=pl.BlockSpec((tm, d), row),
        out_shape=jax.ShapeDtypeStruct((n, d), F32),
        compiler_params=_params(1),
        name="merge_ffn" if merge else "ffn",
    )(*args)


def _gelu_tanh(x):
    return 0.5 * x * (1.0 + jnp.tanh(np.sqrt(2.0 / np.pi).astype(np.float32) * (x + 0.044715 * (x * x * x))))


def _head_mean(sq, bd_ref, head_dim):
    hi = sq.astype(BF16)
    lo = (sq - hi.astype(F32)).astype(BF16)
    return (_dot(hi, bd_ref[...]) + _dot(lo, bd_ref[...])) * (1.0 / head_dim)


def _rope(x, cos, sin_up, sin_dn, shift):
    parts = []
    for j in range(x.shape[1] // LANES):
        xs = x[:, j * LANES:(j + 1) * LANES]
        parts.append(xs * cos + pltpu.roll(xs, shift, 1) * sin_up + pltpu.roll(xs, LANES - shift, 1) * sin_dn)
    return parts[0] if len(parts) == 1 else jnp.concatenate(parts, axis=1)


def _proj_kernel(*refs, per_row, sample, cfg):
    (x_ref, mod_ref, g_ref, w1_ref, w2_ref, w3_ref, sg_ref, wm_ref, sb_ref, qg_ref, kg_ref,
     bdq_ref, bdk_ref, cos_ref, sup_ref, sdn_ref, pa_ref) = refs[:17]
    outs = refs[17:]
    eps, hd, d_a = cfg.eps, cfg.head_dim, cfg.d_a
    nq, nkv, nqi = cfg.n_heads * hd, cfg.n_kv * hd, cfg.n_idx_heads * cfg.idx_dim
    tm = x_ref.shape[0]

    x = x_ref[...]
    h = _adaln(x, g_ref[...], _mod_row(mod_ref, 3, per_row), _mod_row(mod_ref, 4, per_row), eps).astype(BF16)
    z1 = _dot(h, w1_ref[...])
    z2 = _dot(h, w2_ref[...])
    z3 = _dot(h, w3_ref[...])
    o = 0
    zu = z1[:, o:o + d_a]; o += d_a
    zv = z1[:, o:o + d_a]; o += d_a
    zq = z1[:, o:o + nq]; o += nq
    zk = z1[:, o:o + nkv]; o += nkv
    zva = z1[:, o:o + nkv]; o += nkv
    zqi = z1[:, o:o + nqi]

    u = _gelu_tanh(zu)
    gv = _gelu_tanh(zv)
    v = gv * lax.rsqrt(jnp.mean(gv * gv, axis=-1, keepdims=True) + eps) * sg_ref[...]
    vb = v.astype(BF16)
    ck = wm_ref.shape[1]
    rr = lax.broadcasted_iota(I32, (ck, ck), 0)
    cc = lax.broadcasted_iota(I32, (ck, ck), 1)
    lane_group = lax.broadcasted_iota(I32, (ck, d_a), 1) // (d_a // cfg.g_a)
    wms = [jnp.where(rr >= cc, wm_ref[g], 0.0).astype(BF16) for g in range(cfg.g_a)]
    mixed_chunks = []
    for c in range(tm // ck):
        vc = vb[c * ck:(c + 1) * ck, :]
        mixed = sb_ref[...]
        for g in range(cfg.g_a):
            mixed = mixed + jnp.where(lane_group == g, _dot(wms[g], vc), 0.0)
        mixed_chunks.append(mixed)
    mixed = mixed_chunks[0] if len(mixed_chunks) == 1 else jnp.concatenate(mixed_chunks, axis=0)
    o_a = (u * mixed).astype(BF16)
    ga = jax.nn.sigmoid(z3[:, :cfg.d_model])
    gb = jax.nn.sigmoid(z3[:, cfg.d_model:])
    ya = ga * _dot(o_a, pa_ref[...])

    cos, sup, sdn = cos_ref[...], sup_ref[...], sdn_ref[...]
    half = hd // 8
    q = zq * lax.rsqrt(_head_mean(zq * zq, bdq_ref, hd) + eps) * qg_ref[...]
    q = _rope(q, cos, sup, sdn, half)
    k = zk * lax.rsqrt(_head_mean(zk * zk, bdk_ref, hd) + eps) * kg_ref[...]
    k = _rope(k, cos, sup, sdn, half)
    qi = _rope(zqi, cos, sup, sdn, half)
    ki_wide = _rope(z2, cos, sup, sdn, half)
    ki = ki_wide[:, :cfg.idx_dim]
    wi_scale = cfg.n_idx_heads ** -0.5

    if sample:
        (k_ref, va_ref, ki_ref, v_ref, q_ref, qi_ref, wi_ref, ya_ref, gb_ref) = outs
        k_ref[...] = k
        va_ref[...] = zva
        ki_ref[...] = ki
        v_ref[...] = v
        q_ref[...] = q
        qi_ref[...] = qi
        wi_ref[...] = z2[:, cfg.idx_dim:cfg.idx_dim + cfg.n_idx_heads] * wi_scale
    else:
        (kt_ref, vat_ref, kit_ref, qt_ref, qit_ref, kh_ref, vt_ref, kib_ref, wi_ref, ya_ref, gb_ref) = outs
        kt_ref[0] = k.T
        vat = zva.T
        vat_ref[0] = vat
        kit_ref[0] = ki_wide.T[:cfg.idx_dim, :]
        qt_ref[0] = (q * (LOG2_E * hd ** -0.5)).T.astype(BF16)
        qit_ref[0] = qi.T.astype(BF16)
        vt_ref[0] = vat.astype(BF16)
        for hh in range(cfg.n_kv):
            kh_ref[0, hh] = k[:, hh * hd:(hh + 1) * hd].astype(BF16)
        kib_ref[0] = ki.astype(BF16)
        wi_ref[0] = z2.T[cfg.idx_dim:cfg.idx_dim + cfg.n_idx_heads, :] * wi_scale
    ya_ref[...] = ya
    gb_ref[...] = gb


def _rope_tables(pos, cfg):
    hd = cfg.head_dim
    rd = hd // 4
    half = rd // 2
    freqs = cfg.rope_theta ** (-jnp.arange(half, dtype=F32) * 2.0 / rd)
    ang = pos.astype(F32)[:, None] * freqs[None, :]
    cos, sin = jnp.cos(ang), jnp.sin(ang)
    n = pos.shape[0]
    one = jnp.ones((n, hd - rd), F32)
    zero = jnp.zeros((n, hd - rd), F32)
    zh = jnp.zeros((n, half), F32)
    cos_h = jnp.concatenate([cos, cos, one], axis=1)
    sup_h = jnp.concatenate([zh, sin, zero], axis=1)
    sdn_h = jnp.concatenate([-sin, zh, zero], axis=1)
    rep = LANES // hd
    return tuple(jnp.tile(t, (1, rep)) for t in (cos_h, sup_h, sdn_h))


def _proj_weights(lw, cfg):
    d_a, hd = cfg.d_a, cfg.head_dim
    nq, nkv, nqi = cfg.n_heads * hd, cfg.n_kv * hd, cfg.n_idx_heads * cfg.idx_dim
    n1 = 2 * d_a + nq + 2 * nkv + nqi
    n2 = cfg.idx_dim + cfg.n_idx_heads
    w_in = lw["w_in"]
    w1 = w_in[:, :n1].astype(BF16)
    w2 = jnp.pad(w_in[:, n1:n1 + n2], ((0, 0), (0, LANES - n2))).astype(BF16)
    w3 = w_in[:, n1 + n2:].astype(BF16)
    gd = d_a // cfg.g_a

    def block_diag(width):
        hidx = jnp.arange(width) // hd
        return (hidx[:, None] == hidx[None, :]).astype(BF16)

    return dict(
        w1=w1, w2=w2, w3=w3,
        sg=lw["sgu_norm_g"].reshape(1, d_a),
        qg=jnp.tile(lw["q_norm_g"], cfg.n_heads).reshape(1, nq),
        kg=jnp.tile(lw["k_norm_g"], cfg.n_kv).reshape(1, nkv),
        bdq=block_diag(nq), bdk=block_diag(nkv),
        pa=lw["w_branch_a"].astype(BF16),
        wm_p=lw["sgu_w"],
        sb_p=jnp.repeat(lw["sgu_b"].T, gd, axis=1),
    )


def _sample_gate_weights(lw, cfg, n_seq):
    t = cfg.dec_seq
    gd = cfg.d_a // cfg.g_a
    eye = jnp.eye(n_seq, dtype=F32)
    wm = jax.vmap(lambda w: jnp.kron(eye, w))(lw["sgu_w"][:, :t, :t])
    sb = jnp.tile(jnp.repeat(lw["sgu_b"][:, :t].T, gd, axis=1), (n_seq, 1))
    return wm, sb


def _proj(x, mod, g, pw, wm, sb, tabs, cfg, *, tm, rows_per_mod, sample, n_batch=None):
    n, d = x.shape
    per_row = rows_per_mod is None
    row = lambda i: (i, 0)
    if per_row:
        mod_spec = pl.BlockSpec((mod.shape[0], tm, d), lambda i: (0, i, 0))
        tab_spec = pl.BlockSpec((tm, LANES), row)
    else:
        tiles_per_mod = rows_per_mod // tm
        mod_spec = pl.BlockSpec((1, mod.shape[1], d), lambda i: (i // tiles_per_mod, 0, 0))
        tab_spec = pl.BlockSpec((tm, LANES), lambda i: (i % tiles_per_mod, 0))
    consts = [g.reshape(1, d), pw["w1"], pw["w2"], pw["w3"], pw["sg"], wm, sb, pw["qg"], pw["kg"],
              pw["bdq"], pw["bdk"]]
    in_specs = ([pl.BlockSpec((tm, d), row), mod_spec] + [_const_spec(a.shape) for a in consts]
                + [tab_spec] * 3 + [_const_spec(pw["pa"].shape)])
    args = [x, mod] + consts + list(tabs) + [pw["pa"]]
    hd = cfg.head_dim
    nq, nkv, nqi = cfg.n_heads * hd, cfg.n_kv * hd, cfg.n_idx_heads * cfg.idx_dim

    def rows(width, dtype=F32):
        return pl.BlockSpec((tm, width), row), jax.ShapeDtypeStruct((n, width), dtype)

    if sample:
        outs = [rows(nkv), rows(nkv), rows(cfg.idx_dim), rows(cfg.d_a), rows(nq), rows(nqi),
                rows(cfg.n_idx_heads)]
    else:
        s = rows_per_mod
        tps = s // tm

        def transposed(width, dtype=BF16):
            return (pl.BlockSpec((1, width, tm), lambda i: (i // tps, 0, i % tps)),
                    jax.ShapeDtypeStruct((n_batch, width, s), dtype))

        outs = [transposed(nkv, F32), transposed(nkv, F32), transposed(cfg.idx_dim, F32),
                transposed(nq), transposed(nqi),
                 (pl.BlockSpec((1, cfg.n_kv, tm, hd), lambda i: (i // tps, 0, i % tps, 0)),
                  jax.ShapeDtypeStruct((n_batch, cfg.n_kv, s, hd), BF16)),
                 transposed(nkv),
                 (pl.BlockSpec((1, tm, cfg.idx_dim), lambda i: (i // tps, i % tps, 0)),
                  jax.ShapeDtypeStruct((n_batch, s, cfg.idx_dim), BF16)),
                 transposed(cfg.n_idx_heads, F32)]
    outs += [rows(d), rows(d)]
    kern = functools.partial(_proj_kernel, per_row=per_row, sample=sample, cfg=cfg)
    return pl.pallas_call(
        kern,
        grid=(n // tm,),
        in_specs=in_specs,
        out_specs=[o[0] for o in outs],
        out_shape=[o[1] for o in outs],
        compiler_params=_params(1),
        name="proj_sample" if sample else "proj_prompt",
    )(*args)


MASKED = float("-inf")


def _stored_score(x):
    return x + 0.0


def _pattern_score(pattern):
    return lax.bitcast_convert_type(pattern ^ ((pattern >> 31) & 0x7FFFFFFF), F32)


SUBLANES = 8
BF16_ROWS = 16


class _KeyBlock:
    def __init__(self, ref, key_axis, tk, tiles_per_trip=1):
        self.ref, self.key_axis, self.tk, self.tiles_per_trip = ref, key_axis, tk, tiles_per_trip
        self.n_vec = ref.shape[1 - key_axis]
        if key_axis == 1:
            self.unit = (self.n_vec, LANES)
            self.pos = lax.broadcasted_iota(I32, self.unit, 1)
        else:
            self.unit = (SUBLANES, self.n_vec)
            self.pos = lax.broadcasted_iota(I32, (tk, self.n_vec), 0).reshape(
                tk // SUBLANES, SUBLANES, self.n_vec)

    def bcast(self, a):
        return jnp.broadcast_to(a, self.unit)

    def tile(self, off):
        if self.key_axis == 1:
            x = self.ref[:, pl.ds(off, self.tk)]
            return [(x[:, c * LANES:(c + 1) * LANES], self.pos + (off + c * LANES))
                    for c in range(self.tk // LANES)]
        x = self.ref[pl.ds(off, self.tk), :]
        return [(x.reshape(self.tk // SUBLANES, SUBLANES, self.n_vec), self.pos + off)]

    def count(self, n_tiles, pred):
        group = self.tiles_per_trip

        def body(j, acc):
            for u in range(group):
                for x, pos in self.tile(pl.multiple_of((j * group + u) * self.tk, self.tk)):
                    hits = jnp.where(pred(x, pos), 1, 0)
                    acc = acc + (hits if self.key_axis == 1 else jnp.sum(hits, axis=0))
            return acc

        n_trips = n_tiles if group == 1 else (n_tiles + group - 1) // group
        acc = lax.fori_loop(0, n_trips, body, jnp.zeros(self.unit, I32))
        return jnp.sum(acc.astype(F32), axis=self.key_axis, keepdims=True).astype(I32)


def _select_threshold(kb, n_tiles, k_eff, live=None):
    def count_ge(pattern):
        cb = kb.bcast(_pattern_score(pattern))
        return kb.count(n_tiles, lambda x, _: x >= cb)

    c0 = count_ge(jnp.zeros_like(k_eff))
    c1 = kb.count(n_tiles, lambda x, _: x > 0.0)
    t0 = jnp.where(c0 >= k_eff, 0, INT_MIN).astype(I32)
    zero_final = jnp.where(c0 >= k_eff, jnp.where(c1 < k_eff, 1, 0), 0).astype(I32)
    tied0 = jnp.where(c0 > k_eff, zero_final, 0)
    open0 = 1 - zero_final
    if live is not None:
        open0 = jnp.where(live, open0, 0)
        tied0 = jnp.where(live, tied0, 0)

    def any_set(flags):
        return jnp.max(flags.astype(F32)) > 0.5

    def refine(bit, t, opn):
        cand = t + lax.shift_left(jnp.int32(1), bit)
        c = count_ge(cand)
        t = jnp.where(opn > 0, jnp.where(c >= k_eff, cand, t), t)
        return t, jnp.where(c == k_eff, 0, opn)

    def cond(state):
        bit, _, opn = state
        return jnp.logical_and(bit > 0, any_set(opn))

    def body(state):
        bit, t, opn = state
        t, opn = refine(bit, t, opn)
        t, opn = refine(bit - 1, t, opn)
        return bit - 2, t, opn

    t, opn = refine(jnp.int32(30), t0, open0)
    _, t, opn = lax.while_loop(cond, body, (jnp.int32(29), t, opn))
    return _pattern_score(t), any_set(opn + tied0)


def _tie_need(kb, n_tiles, k_eff, t):
    tb = kb.bcast(t)
    return (k_eff - kb.count(n_tiles, lambda x, _: x > tb)).astype(F32)


def _selection_bias(x, t, tie=None):
    if tie is None:
        return jnp.where(x >= t, 0.0, NEG_BIG)
    prefix, need = tie
    return jnp.where(x > t, 0.0, jnp.where(x < t, NEG_BIG, jnp.where(prefix <= need, 0.0, NEG_BIG)))


def _tie_indicator(x, t):
    return jnp.where(x == t, 1.0, 0.0).astype(BF16)


def _softmax_step(s, pv, m_ref, l_ref, acc_ref, idx, key_axis, exp=jnp.exp):
    m_old = m_ref[idx]
    m_new = jnp.maximum(m_old, jnp.max(s, axis=key_axis, keepdims=True))
    p = exp(s - m_new)
    alpha = exp(m_old - m_new)
    l_ref[idx] = alpha * l_ref[idx] + jnp.sum(p, axis=key_axis, keepdims=True)
    acc_ref[idx] = alpha * acc_ref[idx] + pv(p.astype(BF16))
    m_ref[idx] = m_new


def _prompt_attn_kernel(qit_ref, wt_ref, ki_ref, qt_ref, k_ref, vt_ref, o_ref,
                        sc_ref, bias_ref, s_ref, m_ref, acc_ref, *, qb, topk, cfg):
    i = pl.program_id(1)
    tk = qb
    n_tiles = i + 1
    q_pos = i * qb + lax.broadcasted_iota(I32, (1, qb), 1)
    key_pos = lax.broadcasted_iota(I32, (tk, qb), 0)
    wt = wt_ref[0]
    rep = cfg.n_heads // cfg.n_kv
    hd, di = cfg.head_dim, cfg.idx_dim

    def score_tile(j, carry):
        off = pl.multiple_of(j * tk, tk)
        kib = ki_ref[0, pl.ds(off, tk), :]
        acc = jnp.zeros((tk, qb), F32)
        for h in range(cfg.n_idx_heads):
            acc = acc + wt[h:h + 1, :] * jnp.maximum(_dot(kib, qit_ref[0, h * di:(h + 1) * di, :]), 0.0)
        sc_ref[pl.ds(off, tk), :] = jnp.where(key_pos + off <= q_pos, _stored_score(acc), MASKED)
        return carry

    lax.fori_loop(0, n_tiles, score_tile, 0)

    pair = 2 if (sc_ref.shape[0] // tk) % 2 == 0 else 1
    if pair == 2:
        @pl.when(n_tiles % 2 == 1)
        def _():
            sc_ref[pl.ds(pl.multiple_of(n_tiles * tk, tk), tk), :] = jnp.full((tk, qb), MASKED, F32)

    kb = _KeyBlock(sc_ref, 0, tk, tiles_per_trip=pair)
    k_eff = jnp.minimum(q_pos + 1, topk)
    t, has_ties = _select_threshold(kb, n_tiles, k_eff)
    need = lax.cond(has_ties, lambda: _tie_need(kb, n_tiles, k_eff, t), lambda: jnp.zeros((1, qb), F32))

    m_ref[...] = jnp.full(m_ref.shape, NEG_BIG, F32)
    acc_ref[...] = jnp.zeros(acc_ref.shape, F32)
    ones_rows = jnp.where(lax.broadcasted_iota(I32, (BF16_ROWS, tk), 0) == 0, 1.0, 0.0).astype(BF16)
    prefix_mat = jnp.where(key_pos[:, :tk] >= lax.broadcasted_iota(I32, (tk, tk), 1), 1.0, 0.0).astype(BF16)

    def attend_tile(j, tied_before):
        off = pl.multiple_of(j * tk, tk)
        x = sc_ref[pl.ds(off, tk), :]

        def tie_bias(tied_before):
            prefix = _dot(prefix_mat, _tie_indicator(x, t)) + tied_before
            bias_ref[...] = _selection_bias(x, t, (prefix, need))
            return prefix[tk - 1:tk, :]

        def plain_bias(tied_before):
            bias_ref[...] = _selection_bias(x, t)
            return tied_before

        tied_before = lax.cond(has_ties, tie_bias, plain_bias, tied_before)
        bias = jnp.concatenate([bias_ref[...]] * rep, axis=1)
        for g in range(cfg.n_kv):
            qt = jnp.concatenate([qt_ref[0, h * hd:(h + 1) * hd, :] for h in range(g * rep, (g + 1) * rep)],
                                 axis=1)
            s_ref[g] = _dot(k_ref[0, g, pl.ds(off, tk), :], qt) + bias
        for g in range(cfg.n_kv):
            vt = jnp.concatenate([vt_ref[0, g * hd:(g + 1) * hd, pl.ds(off, tk)], ones_rows], axis=0)
            s = s_ref[g]
            m_old = m_ref[g]
            m_new = jnp.maximum(m_old, jnp.max(s, axis=0, keepdims=True))
            p = jnp.exp2(s - m_new).astype(BF16)
            acc_ref[g] = jnp.exp2(m_old - m_new) * acc_ref[g] + _dot(vt, p)
            m_ref[g] = m_new
        return tied_before

    lax.fori_loop(0, n_tiles, attend_tile, jnp.zeros((1, qb), F32))

    heads = []
    for h in range(cfg.n_heads):
        g, r = divmod(h, rep)
        a = acc_ref[g, :, r * qb:(r + 1) * qb]
        heads.append(a[:hd] / a[hd:hd + 1])
    o_ref[0] = jnp.concatenate(heads, axis=0).T.astype(BF16)


def _prompt_attention(qit, wt, kib, qt, kh, vt, cfg, *, qb=256):
    nb, _, s, hd = kh.shape
    qb = min(qb, s)
    topk = min(cfg.topk_max, s // 4)
    rep = cfg.n_heads // cfg.n_kv
    kern = functools.partial(_prompt_attn_kernel, qb=qb, topk=topk, cfg=cfg)
    q_block = lambda b, i: (b, 0, i)

    def per_batch(shape):
        nd = len(shape)
        return pl.BlockSpec((1,) + shape, lambda b, i: (b,) + (0,) * nd, pipeline_mode=pl.Buffered(1))

    return pl.pallas_call(
        kern,
        grid=(nb, s // qb),
        in_specs=[pl.BlockSpec((1, cfg.n_idx_heads * cfg.idx_dim, qb), q_block),
                  pl.BlockSpec((1, cfg.n_idx_heads, qb), q_block),
                  per_batch((s, cfg.idx_dim)),
                  pl.BlockSpec((1, cfg.n_heads * hd, qb), q_block),
                  per_batch((cfg.n_kv, s, hd)),
                  per_batch((cfg.n_kv * hd, s))],
        out_specs=pl.BlockSpec((1, qb, cfg.n_heads * hd), lambda b, i: (b, i, 0)),
        out_shape=jax.ShapeDtypeStruct((nb, s, cfg.n_heads * hd), BF16),
        scratch_shapes=[pltpu.VMEM((s, qb), F32),
                        pltpu.VMEM((qb, qb), F32),
                        pltpu.VMEM((cfg.n_kv, qb, rep * qb), F32),
                        pltpu.VMEM((cfg.n_kv, 1, rep * qb), F32),
                        pltpu.VMEM((cfg.n_kv, hd + BF16_ROWS, rep * qb), F32)],
        compiler_params=_params(2),
        name="prompt_attention",
    )(qit, wt, kib, qt, kh, vt)


TOK_PAD = 8
PAGES_PER_STEP = 32
SEQS_PER_SEARCH = 8


def _samp_index_kernel(pt_ref, qi_ref, w_ref, kit_ref, *refs, n_pages_step, n_chunks, tk, topk, cfg):
    page_refs = refs[:n_pages_step]
    bias_ref, sc_ref = refs[n_pages_step:]
    step = pl.program_id(1)
    seq = step // n_chunks
    c = step % n_chunks
    nh = cfg.n_idx_heads
    qi = qi_ref[0]
    w = w_ref[0]
    width = n_pages_step * cfg.page_size
    rows, total = sc_ref.shape
    seq_rows = pl.ds(pl.multiple_of(seq * TOK_PAD, TOK_PAD), TOK_PAD)

    def token_scores(keys_t):
        y = w * jnp.maximum(_dot(qi, keys_t), 0.0)
        return jnp.sum(y.reshape(nh, TOK_PAD, y.shape[1]), axis=0)

    kc = jnp.concatenate([r[...] for r in page_refs], axis=1).astype(BF16)
    sc_ref[seq_rows, pl.ds(pl.multiple_of(c * width, width), width)] = _stored_score(token_scores(kc))

    @pl.when(c == n_chunks - 1)
    def _():
        tail = kit_ref.shape[2]
        tok = lax.broadcasted_iota(I32, (TOK_PAD, tail), 0)
        new = lax.broadcasted_iota(I32, (TOK_PAD, tail), 1)
        valid = new <= jnp.minimum(tok, cfg.dec_seq - 1)
        sc_ref[seq_rows, total - tail:] = jnp.where(valid, _stored_score(token_scores(kit_ref[0])), MASKED)

    @pl.when(step == pl.num_programs(1) - 1)
    def _():
        tok1 = lax.broadcasted_iota(I32, (rows, 1), 0) % TOK_PAD
        k_eff = jnp.minimum(cfg.past_len + jnp.minimum(tok1, cfg.dec_seq - 1) + 1, topk)
        kb = _KeyBlock(sc_ref, 1, tk)
        t, _ = _select_threshold(kb, total // tk, k_eff, live=tok1 < cfg.dec_seq)
        need = _tie_need(kb, total // tk, k_eff, t)
        prefix_mat = jnp.where(lax.broadcasted_iota(I32, (LANES, LANES), 0)
                               <= lax.broadcasted_iota(I32, (LANES, LANES), 1), 1.0, 0.0).astype(BF16)

        tied_before = jnp.zeros((rows, 1), F32)
        for j in range(total // LANES):
            x = sc_ref[:, j * LANES:(j + 1) * LANES]
            tied = _tie_indicator(x, t)
            prefix = _dot(tied, prefix_mat) + tied_before
            bias_ref[:, j * LANES:(j + 1) * LANES] = _selection_bias(x, t, (prefix, need))
            tied_before = tied_before + jnp.sum(tied.astype(F32), axis=1, keepdims=True)


def _samp_attn_kernel(pt_ref, q_ref, bias_ref, biast_ref, kt_ref, vt_ref, *refs, n_pages_step, cfg):
    k_pages = refs[:n_pages_step]
    v_pages = refs[n_pages_step:2 * n_pages_step]
    o_ref, m_ref, l_ref, acc_ref = refs[2 * n_pages_step:]
    c = pl.program_id(1)
    n_chunks = pl.num_programs(1)
    q = q_ref[0]
    hd = cfg.head_dim
    rep = cfg.n_heads // cfg.n_kv

    @pl.when(c == 0)
    def _():
        m_ref[...] = jnp.full(m_ref.shape, NEG_BIG, F32)
        l_ref[...] = jnp.zeros(l_ref.shape, F32)
        acc_ref[...] = jnp.zeros(acc_ref.shape, F32)

    def step(kt, vt, bias):
        s = _dot(q, kt) + jnp.concatenate([bias] * cfg.n_heads, axis=0)
        _softmax_step(s, lambda p: _dot_nt(p, vt), m_ref, l_ref, acc_ref, 0, 1)

    kc = jnp.concatenate([r[...] for r in k_pages], axis=1).astype(BF16)
    vc = jnp.concatenate([r[...] for r in v_pages], axis=1).astype(BF16)
    step(kc, vc, bias_ref[0])

    @pl.when(c == n_chunks - 1)
    def _():
        step(kt_ref[0], vt_ref[0], biast_ref[0])
        o = acc_ref[0] / l_ref[0]
        for h in range(cfg.n_heads):
            g = h // rep
            o_ref[0, h] = o[h * TOK_PAD:(h + 1) * TOK_PAD, g * hd:(g + 1) * hd]


def _sample_attention(l, q, qi, wi, k_new, v_new, ki_new, cache_k, cache_v, cache_idx_k, page_table, cfg):
    db, n_pages = page_table.shape
    t, hd, ps = cfg.dec_seq, cfg.head_dim, cfg.page_size
    nh, nkv, nih, di = cfg.n_heads, cfg.n_kv, cfg.n_idx_heads, cfg.idx_dim
    rep = nh // nkv
    pstep = min(PAGES_PER_STEP, n_pages)
    n_chunks = n_pages // pstep
    width = pstep * ps
    tail = LANES
    total = cfg.past_len + tail
    topk = min(cfg.topk_max, (cfg.past_len + t) // 4)
    n_lane_groups = total // LANES
    tk = LANES * max(d for d in range(1, 49) if n_lane_groups % d == 0)

    def head_major(a, nheads, dim):
        a = a.reshape(db, t, nheads, dim).transpose(0, 2, 1, 3)
        return jnp.pad(a, ((0, 0), (0, 0), (0, TOK_PAD - t), (0, 0)))

    qi_rows = head_major(qi, nih, di).reshape(db, nih * TOK_PAD, di).astype(BF16)
    w_rows = head_major(wi, nih, 1).reshape(db, nih * TOK_PAD, 1)
    group = (jnp.arange(nh) // rep)[:, None] == jnp.arange(nkv)[None, :]
    q_rows = head_major(q * (hd ** -0.5), nh, hd)[:, :, :, None, :] * group[None, :, None, :, None].astype(F32)
    q_rows = q_rows.reshape(db, nh * TOK_PAD, nkv * hd).astype(BF16)

    def tail_pad(a):
        a = a.reshape(db, t, a.shape[-1]).transpose(0, 2, 1)
        return jnp.pad(a, ((0, 0), (0, 0), (0, tail - t))).astype(BF16)

    ki_tail, k_tail, v_tail = tail_pad(ki_new), tail_pad(k_new), tail_pad(v_new)
    ck = cache_k.transpose(0, 1, 3, 4, 2).reshape(cache_k.shape[:2] + (nkv * hd, ps))
    cv = cache_v.transpose(0, 1, 3, 4, 2).reshape(cache_v.shape[:2] + (nkv * hd, ps))
    cik = cache_idx_k.transpose(0, 1, 3, 2)

    def page_spec(j, w):
        return pl.BlockSpec((None, None, w, ps), lambda b, c, pt: (l, pt[b, c * pstep + j], 0, 0))

    per_seq3 = lambda b, c, pt: (b, 0, 0)
    grp = max(g for g in range(1, SEQS_PER_SEARCH + 1) if db % g == 0)
    grp_seq3 = lambda g, s, pt: (g * grp + s // n_chunks, 0, 0)

    def grp_page_spec(j, w):
        return pl.BlockSpec((None, None, w, ps), lambda g, s, pt: (
            l, pt[g * grp + s // n_chunks, (s % n_chunks) * pstep + j], 0, 0))

    bias = pl.pallas_call(
        functools.partial(_samp_index_kernel, n_pages_step=pstep, n_chunks=n_chunks, tk=tk, topk=topk, cfg=cfg),
        grid_spec=pltpu.PrefetchScalarGridSpec(
            num_scalar_prefetch=1,
            grid=(db // grp, grp * n_chunks),
            in_specs=[pl.BlockSpec((1, nih * TOK_PAD, di), grp_seq3),
                      pl.BlockSpec((1, nih * TOK_PAD, 1), grp_seq3),
                      pl.BlockSpec((1, di, tail), grp_seq3)]
                     + [grp_page_spec(j, di) for j in range(pstep)],
            out_specs=pl.BlockSpec((grp * TOK_PAD, total), lambda g, s, pt: (g, 0)),
            scratch_shapes=[pltpu.VMEM((grp * TOK_PAD, total), F32)]),
        out_shape=jax.ShapeDtypeStruct((db * TOK_PAD, total), F32),
        compiler_params=_params(2),
        name="sample_index",
    )(page_table, qi_rows, w_rows, ki_tail, *([cik] * pstep)).reshape(db, TOK_PAD, total)

    out = pl.pallas_call(
        functools.partial(_samp_attn_kernel, n_pages_step=pstep, cfg=cfg),
        grid_spec=pltpu.PrefetchScalarGridSpec(
            num_scalar_prefetch=1,
            grid=(db, n_chunks),
            in_specs=[pl.BlockSpec((1, nh * TOK_PAD, nkv * hd), per_seq3),
                      pl.BlockSpec((1, TOK_PAD, width), lambda b, c, pt: (b, 0, c)),
                      pl.BlockSpec((1, TOK_PAD, tail), lambda b, c, pt: (b, 0, total // tail - 1)),
                      pl.BlockSpec((1, nkv * hd, tail), per_seq3),
                      pl.BlockSpec((1, nkv * hd, tail), per_seq3)]
                     + [page_spec(j, nkv * hd) for j in range(pstep)] * 2,
            out_specs=pl.BlockSpec((1, nh, TOK_PAD, hd), lambda b, c, pt: (b, 0, 0, 0)),
            scratch_shapes=[pltpu.VMEM((1, nh * TOK_PAD, 1), F32),
                            pltpu.VMEM((1, nh * TOK_PAD, 1), F32),
                            pltpu.VMEM((1, nh * TOK_PAD, nkv * hd), F32)]),
        out_shape=jax.ShapeDtypeStruct((db, nh, TOK_PAD, hd), F32),
        compiler_params=_params(2),
        name="sample_attention",
    )(page_table, q_rows, bias, bias, k_tail, v_tail, *([ck] * pstep), *([cv] * pstep))
    return out[:, :, :t, :].transpose(0, 2, 1, 3).reshape(db * t, nh * hd).astype(BF16)


def _forward(cfg, x_prompt, x_sample, c_prompt, c_sample, cache_k, cache_v, cache_idx_k, page_table,
             mod_w, mod_b, norm_g, ffn_w_in, ffn_w_out, w_in, sgu_norm_g, sgu_w, sgu_b,
             q_norm_g, k_norm_g, w_branch_a, w_branch_b, w_out):
    nb, s, d = x_prompt.shape
    db, t, _ = x_sample.shape
    depth = mod_w.shape[0]
    hd = cfg.head_dim
    n_mod = cfg.n_sub * 3

    n_seq = nb + db
    r_pad = -(-n_seq // 8) * 8
    c_all = jnp.pad(jnp.concatenate([c_prompt, c_sample], axis=0), ((0, r_pad - n_seq), (0, 0)))
    mod = _modulation(c_all, mod_w, mod_b).reshape(depth, r_pad, n_mod, d)
    mod_p = mod[:, :nb]
    mod_s = jnp.repeat(mod[:, nb:n_seq], t, axis=1).transpose(0, 2, 1, 3)

    tabs_p = _rope_tables(jnp.arange(s), cfg)
    tabs_s = _rope_tables(jnp.tile(cfg.past_len + jnp.arange(t), db), cfg)

    xp = x_prompt.reshape(nb * s, d)
    xs = x_sample.reshape(db * t, d)
    tm_ffn = min(512, s)
    tm_proj = min(256, s)
    d_ff = ffn_w_out.shape[2]
    leaves = [[] for _ in range(7)]
    for l in range(depth):
        lw = dict(w_in=w_in[l], sgu_norm_g=sgu_norm_g[l], sgu_w=sgu_w[l], sgu_b=sgu_b[l],
                  q_norm_g=q_norm_g[l], k_norm_g=k_norm_g[l], w_branch_a=w_branch_a[l])
        pw = _proj_weights(lw, cfg)
        wm_s, sb_s = _sample_gate_weights(lw, cfg, db)
        ffn_w = [(ffn_w_in[l, j, :, :d_ff].astype(BF16), ffn_w_in[l, j, :, d_ff:].astype(BF16),
                  ffn_w_out[l, j].astype(BF16)) for j in range(2)]
        pb = w_branch_b[l].astype(BF16)
        wo = w_out[l].astype(BF16)

        xp = _ffn(xp, mod_p[l], norm_g[l, 0], *ffn_w[0], cfg, sub=0, tm=tm_ffn, rows_per_mod=s)
        (k_p, va_p, ki_p, qt, qit, kh, vt, kib, wt, ya_p, gb_p) = _proj(
            xp, mod_p[l], norm_g[l, 1], pw, pw["wm_p"], pw["sb_p"], tabs_p, cfg,
            tm=tm_proj, rows_per_mod=s, sample=False, n_batch=nb)
        ob_p = _prompt_attention(qit, wt, kib, qt, kh, vt, cfg)
        xp = _ffn(xp, mod_p[l], norm_g[l, 2], *ffn_w[1], cfg, sub=2, tm=tm_ffn, rows_per_mod=s,
                  merge_args=(ya_p, gb_p, ob_p.reshape(nb * s, -1), pb, wo))

        xs = _ffn(xs, mod_s[l], norm_g[l, 0], *ffn_w[0], cfg, sub=0, tm=db * t, rows_per_mod=None)
        (k_s, va_s, ki_s, v_s, q_s, qi_s, wi_s, ya_s, gb_s) = _proj(
            xs, mod_s[l], norm_g[l, 1], pw, wm_s, sb_s, tabs_s, cfg,
            tm=db * t, rows_per_mod=None, sample=True)
        ob_s = _sample_attention(l, q_s, qi_s, wi_s, k_s, va_s, ki_s, cache_k, cache_v, cache_idx_k,
                                 page_table, cfg)
        xs = _ffn(xs, mod_s[l], norm_g[l, 2], *ffn_w[1], cfg, sub=2, tm=db * t, rows_per_mod=None,
                  merge_args=(ya_s, gb_s, ob_s, pb, wo))

        leaves[0].append(k_p.reshape(nb, cfg.n_kv, hd, s).transpose(0, 3, 1, 2))
        leaves[1].append(va_p.reshape(nb, cfg.n_kv, hd, s).transpose(0, 3, 1, 2))
        leaves[2].append(ki_p.transpose(0, 2, 1))
        leaves[3].append(k_s.reshape(db, t, cfg.n_kv, hd))
        leaves[4].append(va_s.reshape(db, t, cfg.n_kv, hd))
        leaves[5].append(ki_s.reshape(db, t, cfg.idx_dim))
        leaves[6].append(v_s.reshape(db, t, cfg.d_a))
    return (xp.reshape(nb, s, d), xs.reshape(db, t, d)) + tuple(jnp.stack(v) for v in leaves)


def kernel(x_prompt, x_sample, c_prompt, c_sample, cache_k, cache_v, cache_idx_k, page_table, mod_w, mod_b, norm_g, ffn_w_in, ffn_w_out, w_in, sgu_norm_g, sgu_w, sgu_b, q_norm_g, k_norm_g, w_branch_a, w_branch_b, w_out):
    return _forward(CFG, x_prompt, x_sample, c_prompt, c_sample, cache_k, cache_v, cache_idx_k, page_table,
                    mod_w, mod_b, norm_g, ffn_w_in, ffn_w_out, w_in, sgu_norm_g, sgu_w, sgu_b,
                    q_norm_g, k_norm_g, w_branch_a, w_branch_b, w_out)
```
